```python
import math
import jax, jax.numpy as jnp
from jax import lax
import numpy as np

D_MODEL = 2048
BATCH = 2
SEQ = 4096
DEPTH = 4

N_MIXERS = 2
POOL_WINDOWS = (2, 4, 8, 16)
N_POOL_GROUPS = len(POOL_WINDOWS)
POOL_GROUP = D_MODEL // N_POOL_GROUPS
RET_HEADS = 8
RET_QK_DIM = D_MODEL
RET_V_DIM = 2 * D_MODEL
RET_HEAD_QK = RET_QK_DIM // RET_HEADS
RET_HEAD_V = RET_V_DIM // RET_HEADS
RET_CHUNK = 128
ROPE_BASE = 10000.0
D_FF = 4 * D_MODEL
N_POOL_LAYERS = (DEPTH + 1) // 2
N_RET_LAYERS = DEPTH // 2
EPS = 1e-6
ADA_SCALE = 0.5

kernel_name = 'hybrid_pool_retention_adaln_trunk'


def rmsnorm(x, g):
    xf = x.astype(jnp.float32)
    y = xf * lax.rsqrt(jnp.mean(xf * xf, axis=-1, keepdims=True) + EPS)
    return (y * g.astype(jnp.float32)).astype(x.dtype)


def modulate(h, shift, scale):
    return h * (1 + scale[:, None, :]) + shift[:, None, :]


def pool_mixer(h, w_grp, ls):
    B, S, D = h.shape
    hf = h.astype(jnp.float32)
    cs = jnp.concatenate([jnp.zeros((B, 1, D), jnp.float32), jnp.cumsum(hf, axis=1)], axis=1)
    t = jnp.arange(S)
    outs = []
    for gi, w in enumerate(POOL_WINDOWS):
        lo_c, hi_c = gi * POOL_GROUP, (gi + 1) * POOL_GROUP
        cs_g = cs[:, :, lo_c:hi_c]
        lo = jnp.maximum(t + 1 - w, 0)
        cnt = (t + 1 - lo).astype(jnp.float32)
        mean = (cs_g[:, 1:] - cs_g[:, lo]) / cnt[None, :, None]
        outs.append(mean - hf[:, :, lo_c:hi_c])
    p = jnp.stack(outs, axis=2).astype(h.dtype)
    y = jnp.einsum('bsgc,gcd->bsgd', p, w_grp).reshape(B, S, D)
    return y * ls


def rotary(x, pos):
    d = x.shape[-1]
    inv = ROPE_BASE ** (-jnp.arange(0, d, 2, dtype=jnp.float32) / d)
    ang = pos.astype(jnp.float32)[:, None] * inv[None, :]
    cos = jnp.cos(ang)[None, :, None, :]
    sin = jnp.sin(ang)[None, :, None, :]
    x1, x2 = x[..., 0::2], x[..., 1::2]
    return jnp.stack([x1 * cos - x2 * sin, x1 * sin + x2 * cos], axis=-1).reshape(x.shape)


def retention(h, w_in, w_out):
    B, S, _ = h.shape
    H, dk, dv, C = RET_HEADS, RET_HEAD_QK, RET_HEAD_V, RET_CHUNK
    n = S // C
    proj = h @ w_in
    q, k, v, g = jnp.split(proj, [RET_QK_DIM, 2 * RET_QK_DIM, 2 * RET_QK_DIM + RET_V_DIM], axis=-1)
    pos = jnp.arange(S)
    q = rotary(q.reshape(B, S, H, dk).astype(jnp.float32), pos)
    k = rotary(k.reshape(B, S, H, dk).astype(jnp.float32), pos) * (dk ** -0.5)
    v = v.reshape(B, S, H, dv).astype(jnp.float32)

    gamma = 1.0 - 2.0 ** (-5.0 - jnp.arange(H, dtype=jnp.float32))
    log_g = jnp.log(gamma)
    j = jnp.arange(C, dtype=jnp.float32)
    diff = j[:, None] - j[None, :]
    intra = jnp.where(diff[None] >= 0, jnp.exp(jnp.maximum(diff, 0.0)[None] * log_g[:, None, None]), 0.0)
    q_dec = jnp.exp((j[None, :] + 1.0) * log_g[:, None])[None, :, :, None]
    k_dec = jnp.exp((C - 1.0 - j[None, :]) * log_g[:, None])[None, :, :, None]
    chunk_dec = jnp.exp(C * log_g)[None, :, None, None]

    def to_chunks(a):
        return a.reshape(B, n, C, H, a.shape[-1]).transpose(1, 0, 3, 2, 4)

    def step(state, inp):
        qc, kc, vc = inp
        scores = jnp.einsum('bhid,bhjd->bhij', qc, kc) * intra[None]
        inner = jnp.einsum('bhij,bhje->bhie', scores, vc)
        cross = jnp.einsum('bhid,bhde->bhie', qc * q_dec, state)
        new_state = chunk_dec * state + jnp.einsum('bhjd,bhje->bhde', kc * k_dec, vc)
        return new_state, inner + cross

    state0 = jnp.zeros((B, H, dk, dv), jnp.float32)
    _, y = lax.scan(step, state0, (to_chunks(q), to_chunks(k), to_chunks(v)))
    y = y.transpose(1, 0, 3, 2, 4).reshape(B, S, H, dv)
    mu = jnp.mean(y, axis=-1, keepdims=True)
    var = jnp.mean((y - mu) ** 2, axis=-1, keepdims=True)
    y = ((y - mu) * lax.rsqrt(var + EPS)).reshape(B, S, RET_V_DIM).astype(h.dtype)
    return (jax.nn.silu(g) * y) @ w_out


def sq_relu_mlp(h, w1, w2):
    a = jax.nn.relu(h @ w1)
    return (a * a) @ w2


def setup_inputs(seed: int = 0) -> dict:
    key = jax.random.key(seed)
    ks = jax.random.split(key, 13)
    D = D_MODEL
    nrm = jax.random.normal
    x = nrm(ks[0], (BATCH, SEQ, D), jnp.float32)
    c = nrm(ks[1], (BATCH, D), jnp.float32)
    ada_w = nrm(ks[2], (DEPTH, 2, D, 3 * D), jnp.float32) * (ADA_SCALE * D ** -0.5)
    ada_b = nrm(ks[3], (DEPTH, 2, 3 * D), jnp.float32) * 0.02
    norm_g = 1.0 + 0.05 * nrm(ks[4], (DEPTH, 2, D), jnp.float32)
    pool_w = nrm(ks[5], (N_POOL_LAYERS, N_POOL_GROUPS, POOL_GROUP, POOL_GROUP), jnp.float32) * POOL_GROUP ** -0.5
    pool_scale = 1.0 + 0.1 * nrm(ks[6], (N_POOL_LAYERS, D), jnp.float32)
    ret_w_in = nrm(ks[7], (N_RET_LAYERS, D, 2 * RET_QK_DIM + 2 * RET_V_DIM), jnp.float32) * D ** -0.5
    ret_w_out = nrm(ks[8], (N_RET_LAYERS, RET_V_DIM, D), jnp.float32) * RET_V_DIM ** -0.5
    mlp_w1 = nrm(ks[9], (DEPTH, D, D_FF), jnp.float32) * D ** -0.5
    mlp_w2 = nrm(ks[10], (DEPTH, D_FF, D), jnp.float32) * D_FF ** -0.5
    final_g = 1.0 + 0.05 * nrm(ks[11], (D,), jnp.float32)
    return {'x': x, 'c': c, 'ada_w': ada_w, 'ada_b': ada_b, 'norm_g': norm_g,
            'pool_w': pool_w, 'pool_scale': pool_scale, 'ret_w_in': ret_w_in, 'ret_w_out': ret_w_out,
            'mlp_w1': mlp_w1, 'mlp_w2': mlp_w2, 'final_g': final_g}


def reference(x, c, ada_w, ada_b, norm_g, pool_w, pool_scale, ret_w_in, ret_w_out, mlp_w1, mlp_w2, final_g):
    cond = jax.nn.silu(c)
    for i in range(DEPTH):
        mod = jnp.einsum('bd,sdk->bsk', cond, ada_w[i]) + ada_b[i]
        shift_t, scale_t, gate_t = jnp.split(mod[:, 0], 3, axis=-1)
        shift_m, scale_m, gate_m = jnp.split(mod[:, 1], 3, axis=-1)
        h = modulate(rmsnorm(x, norm_g[i, 0]), shift_t, scale_t)
        if i % N_MIXERS == 0:
            y = pool_mixer(h, pool_w[i // N_MIXERS], pool_scale[i // N_MIXERS])
        else:
            y = retention(h, ret_w_in[i // N_MIXERS], ret_w_out[i // N_MIXERS])
        x = x + gate_t[:, None, :] * y
        h = modulate(rmsnorm(x, norm_g[i, 1]), shift_m, scale_m)
        x = x + gate_m[:, None, :] * sq_relu_mlp(h, mlp_w1[i], mlp_w2[i])
    return rmsnorm(x, final_g)
```

```python
import functools
import math

import jax
import jax.numpy as jnp
from jax import lax
from jax.experimental import pallas as pl
from jax.experimental.pallas import tpu as pltpu

D_MODEL = 2048
DEPTH = 4
POOL_WINDOWS = (2, 4, 8, 16)
POOL_GROUP = D_MODEL // len(POOL_WINDOWS)
POOL_HALO = 16
RET_HEADS = 8
RET_QK_DIM = D_MODEL
RET_V_DIM = 2 * D_MODEL
RET_HEAD_QK = RET_QK_DIM // RET_HEADS
RET_HEAD_V = RET_V_DIM // RET_HEADS
RET_PROJ = 2 * RET_QK_DIM + 2 * RET_V_DIM
ROPE_BASE = 10000.0
D_FF = 4 * D_MODEL
EPS = 1e-6

V7X_VMEM_BYTES = 64 * 1024 * 1024
VMEM_LIMIT_BYTES = V7X_VMEM_BYTES - 8 * 1024 * 1024

ADA_TK = 1024
POOL_TM = 512
MLP_TM = 1024
MLP_TF = 512
MLP_TN = 512
PROJ_TM = 1024
PROJ_TN = 1024
RET_TS = 512
RET_CHUNK = 128
OUT_TM = 1024
OUT_TN = 512


def _params(*semantics):
    return pltpu.CompilerParams(dimension_semantics=semantics, vmem_limit_bytes=VMEM_LIMIT_BYTES)


def _norm_modulate(x, g, mod_row):
    shift = mod_row[:, 0:D_MODEL]
    scale = mod_row[:, D_MODEL:2 * D_MODEL]
    inv = lax.rsqrt(jnp.mean(x * x, axis=-1, keepdims=True) + EPS)
    return x * inv * (g * (1.0 + scale)) + shift


def _ada_kernel(c_ref, w_ref, b_ref, o_ref):
    cond = jax.nn.silu(c_ref[...]).astype(jnp.bfloat16)
    w = w_ref[...].astype(jnp.bfloat16)
    o_ref[...] = jnp.dot(cond, w, preferred_element_type=jnp.float32) + b_ref[...]


def _ada_modulation(c_pad, ada_w, ada_b):
    n_ls = ada_w.shape[0]
    n_out = ada_w.shape[2]
    return pl.pallas_call(
        _ada_kernel,
        out_shape=jax.ShapeDtypeStruct((n_ls, 8, n_out), jnp.float32),
        grid=(n_ls, n_out // ADA_TK),
        in_specs=[
            pl.BlockSpec((8, D_MODEL), lambda l, j: (0, 0)),
            pl.BlockSpec((None, D_MODEL, ADA_TK), lambda l, j: (l, 0, j)),
            pl.BlockSpec((None, 1, ADA_TK), lambda l, j: (l, 0, j)),
        ],
        out_specs=pl.BlockSpec((None, 8, ADA_TK), lambda l, j: (l, 0, j)),
        compiler_params=_params("parallel", "parallel"),
        name="ada_modulation",
    )(c_pad, ada_w, ada_b)


def _pool_kernel(tiles_per_seq, x_ref, halo_ref, mod_ref, g_ref, w_ref, ls_ref, o_ref):
    i = pl.program_id(0)
    tile_in_seq = i % tiles_per_seq
    tm = x_ref.shape[0]
    mod_row = mod_ref[0]
    g = g_ref[...]
    x = x_ref[...]
    h = _norm_modulate(x, g, mod_row)
    h_halo = _norm_modulate(halo_ref[...], g, mod_row)
    h_halo = jnp.where(tile_in_seq == 0, 0.0, h_halo)
    gate = mod_row[:, 2 * D_MODEL:3 * D_MODEL]
    pos = tile_in_seq * tm + lax.broadcasted_iota(jnp.int32, (tm, 1), 0)
    for gi, window in enumerate(POOL_WINDOWS):
        cols = slice(gi * POOL_GROUP, (gi + 1) * POOL_GROUP)
        hg = h[:, cols]
        acc = jnp.concatenate([h_halo[:, cols], hg], axis=0)
        shift = 1
        while shift < window:
            acc = acc + pltpu.roll(acc, shift, axis=0)
            shift *= 2
        cnt = jnp.minimum(pos + 1, window).astype(jnp.float32)
        p = acc[POOL_HALO:, :] / cnt - hg
        y = jnp.dot(p.astype(jnp.bfloat16), w_ref[gi], preferred_element_type=jnp.float32)
        o_ref[:, cols] = x[:, cols] + gate[:, cols] * (y * ls_ref[:, cols])


def _pool_layer(x2d, mod, mod_idx, seq, norm_g, pool_w, pool_scale):
    n_tok = x2d.shape[0]
    tiles_per_seq = seq // POOL_TM
    halo_blocks_per_tile = POOL_TM // POOL_HALO
    return pl.pallas_call(
        functools.partial(_pool_kernel, tiles_per_seq),
        out_shape=jax.ShapeDtypeStruct(x2d.shape, jnp.float32),
        grid=(n_tok // POOL_TM,),
        in_specs=[
            pl.BlockSpec((POOL_TM, D_MODEL), lambda i: (i, 0)),
            pl.BlockSpec((POOL_HALO, D_MODEL), lambda i: (jnp.maximum(i * halo_blocks_per_tile - 1, 0), 0)),
            pl.BlockSpec((1, 1, 3 * D_MODEL), lambda i: (mod_idx + i // tiles_per_seq, 0, 0)),
            pl.BlockSpec((1, D_MODEL), lambda i: (0, 0)),
            pl.BlockSpec((len(POOL_WINDOWS), POOL_GROUP, POOL_GROUP), lambda i: (0, 0, 0)),
            pl.BlockSpec((1, D_MODEL), lambda i: (0, 0)),
        ],
        out_specs=pl.BlockSpec((POOL_TM, D_MODEL), lambda i: (i, 0)),
        compiler_params=_params("parallel"),
        name="pool_mixer",
    )(x2d, x2d, mod, norm_g, pool_w, pool_scale)


def _mlp_kernel(final_norm, x_ref, mod_ref, g_ref, w1_ref, w2_ref, fg_ref, o_ref, h_ref):
    j = pl.program_id(1)

    @pl.when(j == 0)
    def _():
        h_ref[...] = _norm_modulate(x_ref[...], g_ref[...], mod_ref[0]).astype(jnp.bfloat16)

    a = jnp.dot(h_ref[...], w1_ref[...], preferred_element_type=jnp.float32)
    a = jnp.maximum(a, 0.0)
    a = (a * a).astype(jnp.bfloat16)
    for n in range(0, D_MODEL, MLP_TN):
        part = jnp.dot(a, w2_ref[:, n:n + MLP_TN], preferred_element_type=jnp.float32)

        @pl.when(j == 0)
        def _():
            o_ref[:, n:n + MLP_TN] = part

        @pl.when(j > 0)
        def _():
            o_ref[:, n:n + MLP_TN] += part

    @pl.when(j == pl.num_programs(1) - 1)
    def _():
        gate = mod_ref[0][:, 2 * D_MODEL:3 * D_MODEL]
        y = x_ref[...] + gate * o_ref[...]
        if final_norm:
            inv = lax.rsqrt(jnp.mean(y * y, axis=-1, keepdims=True) + EPS)
            y = y * inv * fg_ref[...]
        o_ref[...] = y


def _mlp_layer(x2d, mod, mod_idx, seq, norm_g, w1, w2, final_g, final_norm):
    n_tok = x2d.shape[0]
    tiles_per_seq = seq // MLP_TM
    return pl.pallas_call(
        functools.partial(_mlp_kernel, final_norm),
        out_shape=jax.ShapeDtypeStruct(x2d.shape, jnp.float32),
        grid=(n_tok // MLP_TM, D_FF // MLP_TF),
        in_specs=[
            pl.BlockSpec((MLP_TM, D_MODEL), lambda i, j: (i, 0)),
            pl.BlockSpec((1, 1, 3 * D_MODEL), lambda i, j: (mod_idx + i // tiles_per_seq, 0, 0)),
            pl.BlockSpec((1, D_MODEL), lambda i, j: (0, 0)),
            pl.BlockSpec((D_MODEL, MLP_TF), lambda i, j: (0, j)),
            pl.BlockSpec((MLP_TF, D_MODEL), lambda i, j: (j, 0)),
            pl.BlockSpec((1, D_MODEL), lambda i, j: (0, 0)),
        ],
        out_specs=pl.BlockSpec((MLP_TM, D_MODEL), lambda i, j: (i, 0)),
        scratch_shapes=[pltpu.VMEM((MLP_TM, D_MODEL), jnp.bfloat16)],
        compiler_params=_params("parallel", "arbitrary"),
        name="mlp_final" if final_norm else "mlp",
    )(x2d, mod, norm_g, w1, w2, final_g)


def _proj_kernel(x_ref, mod_ref, g_ref, w_ref, cos_ref, sin_ref, o_ref, h_ref):
    j = pl.program_id(1)
    n_q = RET_QK_DIM // PROJ_TN
    half = RET_HEAD_QK // 2

    @pl.when(j == 0)
    def _():
        h_ref[...] = _norm_modulate(x_ref[...], g_ref[...], mod_ref[0]).astype(jnp.bfloat16)

    r = jnp.dot(h_ref[...], w_ref[...], preferred_element_type=jnp.float32)

    @pl.when(j < 2 * n_q)
    def _():
        k_scale = jnp.where(j >= n_q, RET_HEAD_QK ** -0.5, 1.0)
        cos = cos_ref[...] * k_scale
        sin = sin_ref[...] * k_scale
        for hd in range(PROJ_TN // RET_HEAD_QK):
            lo = hd * RET_HEAD_QK
            x1 = r[:, lo:lo + half]
            x2 = r[:, lo + half:lo + 2 * half]
            o_ref[:, lo:lo + half] = (x1 * cos - x2 * sin).astype(o_ref.dtype)
            o_ref[:, lo + half:lo + 2 * half] = (x1 * sin + x2 * cos).astype(o_ref.dtype)

    @pl.when(j >= 2 * n_q)
    def _():
        o_ref[...] = r.astype(o_ref.dtype)


def _ret_projection(x2d, mod, mod_idx, seq, norm_g, w_in, cos, sin):
    n_tok = x2d.shape[0]
    tiles_per_seq = seq // PROJ_TM
    return pl.pallas_call(
        _proj_kernel,
        out_shape=jax.ShapeDtypeStruct((n_tok, RET_PROJ), jnp.bfloat16),
        grid=(n_tok // PROJ_TM, RET_PROJ // PROJ_TN),
        in_specs=[
            pl.BlockSpec((PROJ_TM, D_MODEL), lambda i, j: (i, 0)),
            pl.BlockSpec((1, 1, 3 * D_MODEL), lambda i, j: (mod_idx + i // tiles_per_seq, 0, 0)),
            pl.BlockSpec((1, D_MODEL), lambda i, j: (0, 0)),
            pl.BlockSpec((D_MODEL, PROJ_TN), lambda i, j: (0, j)),
            pl.BlockSpec((PROJ_TM, RET_HEAD_QK // 2), lambda i, j: (i % tiles_per_seq, 0)),
            pl.BlockSpec((PROJ_TM, RET_HEAD_QK // 2), lambda i, j: (i % tiles_per_seq, 0)),
        ],
        out_specs=pl.BlockSpec((PROJ_TM, PROJ_TN), lambda i, j: (i, j)),
        scratch_shapes=[pltpu.VMEM((PROJ_TM, D_MODEL), jnp.bfloat16)],
        compiler_params=_params("parallel", "arbitrary"),
        name="ret_projection",
    )(x2d, mod, norm_g, w_in, cos, sin)


def _scan_kernel(q_ref, k_ref, v_ref, g_ref, intra_ref, qdec_ref, kdec_ref, cdec_ref, o_ref, state_ref):
    @pl.when(pl.program_id(2) == 0)
    def _():
        state_ref[...] = jnp.zeros_like(state_ref)

    intra = intra_ref[...]
    q_dec = qdec_ref[...]
    k_dec = kdec_ref[...]
    chunk_dec = cdec_ref[...]
    for c in range(q_ref.shape[0] // RET_CHUNK):
        rows = slice(c * RET_CHUNK, (c + 1) * RET_CHUNK)
        qc = q_ref[rows, :]
        kc = k_ref[rows, :]
        vc = v_ref[rows, :]
        state = state_ref[...]
        scores = lax.dot_general(qc, kc, (((1,), (1,)), ((), ())), preferred_element_type=jnp.float32)
        scores = (scores * intra).astype(jnp.bfloat16)
        inner = jnp.dot(scores, vc, preferred_element_type=jnp.float32)
        cross = jnp.dot(qc, state.astype(jnp.bfloat16), preferred_element_type=jnp.float32) * q_dec
        k_decayed = (kc.astype(jnp.float32) * k_dec).astype(jnp.bfloat16)
        state_ref[...] = chunk_dec * state + lax.dot_general(
            k_decayed, vc, (((0,), (0,)), ((), ())), preferred_element_type=jnp.float32)
        y = inner + cross
        mu = jnp.mean(y, axis=-1, keepdims=True)
        yc = y - mu
        var = jnp.mean(yc * yc, axis=-1, keepdims=True)
        yn = yc * lax.rsqrt(var + EPS)
        o_ref[rows, :] = (jax.nn.silu(g_ref[rows, :].astype(jnp.float32)) * yn).astype(o_ref.dtype)


def _ret_scan(proj, batch, seq, intra, q_dec, k_dec, chunk_dec):
    n_tok = proj.shape[0]
    steps = seq // RET_TS
    k_off = RET_QK_DIM // RET_HEAD_QK
    v_off = 2 * RET_QK_DIM // RET_HEAD_V
    g_off = v_off + RET_HEADS
    return pl.pallas_call(
        _scan_kernel,
        out_shape=jax.ShapeDtypeStruct((n_tok, RET_V_DIM), jnp.bfloat16),
        grid=(batch, RET_HEADS, steps),
        in_specs=[
            pl.BlockSpec((RET_TS, RET_HEAD_QK), lambda b, h, t: (b * steps + t, h)),
            pl.BlockSpec((RET_TS, RET_HEAD_QK), lambda b, h, t: (b * steps + t, k_off + h)),
            pl.BlockSpec((RET_TS, RET_HEAD_V), lambda b, h, t: (b * steps + t, v_off + h)),
            pl.BlockSpec((RET_TS, RET_HEAD_V), lambda b, h, t: (b * steps + t, g_off + h)),
            pl.BlockSpec((None, RET_CHUNK, RET_CHUNK), lambda b, h, t: (h, 0, 0)),
            pl.BlockSpec((None, RET_CHUNK, 1), lambda b, h, t: (h, 0, 0)),
            pl.BlockSpec((None, RET_CHUNK, 1), lambda b, h, t: (h, 0, 0)),
            pl.BlockSpec((None, 1, 1), lambda b, h, t: (h, 0, 0)),
        ],
        out_specs=pl.BlockSpec((RET_TS, RET_HEAD_V), lambda b, h, t: (b * steps + t, h)),
        scratch_shapes=[pltpu.VMEM((RET_HEAD_QK, RET_HEAD_V), jnp.float32)],
        compiler_params=_params("parallel", "parallel", "arbitrary"),
        name="ret_scan",
    )(proj, proj, proj, proj, intra, q_dec, k_dec, chunk_dec)


def _out_kernel(z_ref, w_ref, x_ref, gate_ref, o_ref):
    y = jnp.dot(z_ref[...], w_ref[...], preferred_element_type=jnp.float32)
    o_ref[...] = x_ref[...] + gate_ref[0] * y


def _ret_output(z, x2d, mod, mod_idx, seq, w_out):
    n_tok = x2d.shape[0]
    tiles_per_seq = seq // OUT_TM
    gate_off = 2 * D_MODEL // OUT_TN
    return pl.pallas_call(
        _out_kernel,
        out_shape=jax.ShapeDtypeStruct(x2d.shape, jnp.float32),
        grid=(n_tok // OUT_TM, D_MODEL // OUT_TN),
        in_specs=[
            pl.BlockSpec((OUT_TM, RET_V_DIM), lambda i, j: (i, 0)),
            pl.BlockSpec((RET_V_DIM, OUT_TN), lambda i, j: (0, j)),
            pl.BlockSpec((OUT_TM, OUT_TN), lambda i, j: (i, j)),
            pl.BlockSpec((1, 1, OUT_TN), lambda i, j: (mod_idx + i // tiles_per_seq, 0, gate_off + j)),
        ],
        out_specs=pl.BlockSpec((OUT_TM, OUT_TN), lambda i, j: (i, j)),
        compiler_params=_params("parallel", "arbitrary"),
        name="ret_output",
    )(z, w_out, x2d, mod)


def _rotary_tables(seq):
    half = RET_HEAD_QK // 2
    inv = ROPE_BASE ** (-jnp.arange(0, RET_HEAD_QK, 2, dtype=jnp.float32) / RET_HEAD_QK)
    ang = jnp.arange(seq, dtype=jnp.float32)[:, None] * inv[None, :]
    assert ang.shape == (seq, half)
    return jnp.cos(ang), jnp.sin(ang)


def _decay_tables():
    gamma = 1.0 - 2.0 ** (-5.0 - jnp.arange(RET_HEADS, dtype=jnp.float32))
    log_g = jnp.log(gamma)
    j = jnp.arange(RET_CHUNK, dtype=jnp.float32)
    diff = j[:, None] - j[None, :]
    intra = jnp.where(diff[None] >= 0, jnp.exp(jnp.maximum(diff, 0.0)[None] * log_g[:, None, None]), 0.0)
    q_dec = jnp.exp((j[None, :] + 1.0) * log_g[:, None])[:, :, None]
    k_dec = jnp.exp((RET_CHUNK - 1.0 - j[None, :]) * log_g[:, None])[:, :, None]
    chunk_dec = jnp.exp(RET_CHUNK * log_g)[:, None, None]
    return intra, q_dec, k_dec, chunk_dec


def _pair_split_columns(w):
    d_in = w.shape[0]
    n_qk = 2 * RET_QK_DIM
    qk = w[:, :n_qk].reshape(d_in, n_qk // RET_HEAD_QK, RET_HEAD_QK // 2, 2)
    qk = qk.transpose(0, 1, 3, 2).reshape(d_in, n_qk)
    return jnp.concatenate([qk, w[:, n_qk:]], axis=1)


@jax.jit
def kernel(x, c, ada_w, ada_b, norm_g, pool_w, pool_scale, ret_w_in, ret_w_out, mlp_w1, mlp_w2, final_g):
    batch, seq, d = x.shape
    assert d == D_MODEL and batch <= 8
    assert seq % max(POOL_TM, MLP_TM, PROJ_TM, OUT_TM, RET_TS) == 0
    bf16 = jnp.bfloat16

    c_pad = jnp.zeros((8, d), jnp.float32).at[:batch].set(c)
    mod = _ada_modulation(c_pad, ada_w.reshape(DEPTH * 2, d, 3 * d), ada_b.reshape(DEPTH * 2, 1, 3 * d))
    mod = mod[:, :batch].reshape(DEPTH * 2 * batch, 1, 3 * d)

    cos, sin = _rotary_tables(seq)
    intra, q_dec, k_dec, chunk_dec = _decay_tables()
    final_row = final_g.reshape(1, d)

    x2d = x.reshape(batch * seq, d)
    for i in range(DEPTH):
        mix_idx = (2 * i) * batch
        mlp_idx = (2 * i + 1) * batch
        if i % 2 == 0:
            x2d = _pool_layer(x2d, mod, mix_idx, seq, norm_g[i, 0].reshape(1, d),
                              pool_w[i // 2].astype(bf16), pool_scale[i // 2].reshape(1, d))
        else:
            w_in = _pair_split_columns(ret_w_in[i // 2]).astype(bf16)
            proj = _ret_projection(x2d, mod, mix_idx, seq, norm_g[i, 0].reshape(1, d), w_in, cos, sin)
            z = _ret_scan(proj, batch, seq, intra, q_dec, k_dec, chunk_dec)
            x2d = _ret_output(z, x2d, mod, mix_idx, seq, ret_w_out[i // 2].astype(bf16))
        x2d = _mlp_layer(x2d, mod, mlp_idx, seq, norm_g[i, 1].reshape(1, d),
                         mlp_w1[i].astype(bf16), mlp_w2[i].astype(bf16), final_row,
                         final_norm=(i == DEPTH - 1))
    return x2d.reshape(batch, seq, d)
```

```python
import functools

import jax
import jax.numpy as jnp
from jax import lax
from jax.experimental import pallas as pl
from jax.experimental.pallas import tpu as pltpu

D_MODEL = 2048
DEPTH = 4
POOL_WINDOWS = (2, 4, 8, 16)
POOL_GROUP = D_MODEL // len(POOL_WINDOWS)
POOL_HALO = 16
RET_HEADS = 8
RET_QK_DIM = D_MODEL
RET_V_DIM = 2 * D_MODEL
RET_HEAD_QK = RET_QK_DIM // RET_HEADS
RET_HEAD_V = RET_V_DIM // RET_HEADS
RET_PROJ = 2 * RET_QK_DIM + 2 * RET_V_DIM
ROPE_BASE = 10000.0
D_FF = 4 * D_MODEL
EPS = 1e-6

V7X_VMEM_BYTES = 64 * 1024 * 1024
VMEM_LIMIT_BYTES = V7X_VMEM_BYTES - 4 * 1024 * 1024

ADA_TK = 1024
POOL_TM = 512
MLP_TM = 1024
MLP_TF = 512
MLP_TN = 512
PROJ_TM = 1024
PROJ_TN = 1024
RET_TS = 512
RET_CHUNK = 128
OUT_TM = 1024
OUT_TN = 512


def _params(*semantics):
    return pltpu.CompilerParams(dimension_semantics=semantics, vmem_limit_bytes=VMEM_LIMIT_BYTES)


def _norm_modulate(x, g, mod_row):
    shift = mod_row[:, 0:D_MODEL]
    scale = mod_row[:, D_MODEL:2 * D_MODEL]
    inv = lax.rsqrt(jnp.mean(x * x, axis=-1, keepdims=True) + EPS)
    return x * inv * (g * (1.0 + scale)) + shift


def _ada_kernel(c_ref, w_ref, b_ref, o_ref):
    cond = jax.nn.silu(c_ref[...]).astype(jnp.bfloat16)
    w = w_ref[...].astype(jnp.bfloat16)
    o_ref[...] = jnp.dot(cond, w, preferred_element_type=jnp.float32) + b_ref[...]


def _ada_modulation(c_pad, ada_w, ada_b):
    n_ls = ada_w.shape[0]
    n_out = ada_w.shape[2]
    return pl.pallas_call(
        _ada_kernel,
        out_shape=jax.ShapeDtypeStruct((n_ls, 8, n_out), jnp.float32),
        grid=(n_ls, n_out // ADA_TK),
        in_specs=[
            pl.BlockSpec((8, D_MODEL), lambda l, j: (0, 0)),
            pl.BlockSpec((None, D_MODEL, ADA_TK), lambda l, j: (l, 0, j)),
            pl.BlockSpec((None, 1, ADA_TK), lambda l, j: (l, 0, j)),
        ],
        out_specs=pl.BlockSpec((None, 8, ADA_TK), lambda l, j: (l, 0, j)),
        compiler_params=_params("parallel", "parallel"),
        name="ada_modulation",
    )(c_pad, ada_w, ada_b)


def _pool_kernel(tiles_per_seq, x_ref, halo_ref, mod_ref, g_ref, w_ref, ls_ref, o_ref, wb_ref):
    i = pl.program_id(0)

    @pl.when(i == 0)
    def _():
        wb_ref[...] = w_ref[...].astype(jnp.bfloat16)

    tile_in_seq = i % tiles_per_seq
    tm = x_ref.shape[0]
    mod_row = mod_ref[0]
    g = g_ref[0]
    x = x_ref[...]
    h = _norm_modulate(x, g, mod_row)
    h_halo = _norm_modulate(halo_ref[...], g, mod_row)
    h_halo = jnp.where(tile_in_seq == 0, 0.0, h_halo)
    gate = mod_row[:, 2 * D_MODEL:3 * D_MODEL]
    ls = ls_ref[0]
    pos = tile_in_seq * tm + lax.broadcasted_iota(jnp.int32, (tm, 1), 0)
    for gi, window in enumerate(POOL_WINDOWS):
        cols = slice(gi * POOL_GROUP, (gi + 1) * POOL_GROUP)
        hg = h[:, cols]
        acc = jnp.concatenate([h_halo[:, cols], hg], axis=0)
        shift = 1
        while shift < window:
            acc = acc + pltpu.roll(acc, shift, axis=0)
            shift *= 2
        cnt = jnp.minimum(pos + 1, window).astype(jnp.float32)
        p = acc[POOL_HALO:, :] / cnt - hg
        y = jnp.dot(p.astype(jnp.bfloat16), wb_ref[gi], preferred_element_type=jnp.float32)
        o_ref[:, cols] = x[:, cols] + gate[:, cols] * (y * ls[:, cols])


def _pool_layer(x2d, mod, mod_idx, seq, norm_g, norm_idx, pool_w, pool_scale, pool_idx):
    n_tok = x2d.shape[0]
    tiles_per_seq = seq // POOL_TM
    halo_blocks_per_tile = POOL_TM // POOL_HALO
    n_grp = len(POOL_WINDOWS)
    return pl.pallas_call(
        functools.partial(_pool_kernel, tiles_per_seq),
        out_shape=jax.ShapeDtypeStruct(x2d.shape, jnp.float32),
        grid=(n_tok // POOL_TM,),
        in_specs=[
            pl.BlockSpec((POOL_TM, D_MODEL), lambda i: (i, 0)),
            pl.BlockSpec((POOL_HALO, D_MODEL), lambda i: (jnp.maximum(i * halo_blocks_per_tile - 1, 0), 0)),
            pl.BlockSpec((1, 1, 3 * D_MODEL), lambda i: (mod_idx + i // tiles_per_seq, 0, 0)),
            pl.BlockSpec((1, 1, D_MODEL), lambda i: (norm_idx, 0, 0)),
            pl.BlockSpec((None, n_grp, POOL_GROUP, POOL_GROUP), lambda i: (pool_idx, 0, 0, 0)),
            pl.BlockSpec((1, 1, D_MODEL), lambda i: (pool_idx, 0, 0)),
        ],
        out_specs=pl.BlockSpec((POOL_TM, D_MODEL), lambda i: (i, 0)),
        scratch_shapes=[pltpu.VMEM((n_grp, POOL_GROUP, POOL_GROUP), jnp.bfloat16)],
        compiler_params=_params("arbitrary"),
        name="pool_mixer",
    )(x2d, x2d, mod, norm_g, pool_w, pool_scale)


def _mlp_kernel(final_norm, x_ref, mod_ref, g_ref, w1_ref, w2_ref, fg_ref, o_ref, h_ref):
    j = pl.program_id(1)

    @pl.when(j == 0)
    def _():
        x = x_ref[...]
        h_ref[...] = _norm_modulate(x, g_ref[0], mod_ref[0]).astype(jnp.bfloat16)
        o_ref[...] = x

    a = jnp.dot(h_ref[...], w1_ref[...].astype(jnp.bfloat16), preferred_element_type=jnp.float32)
    a = jnp.maximum(a, 0.0)
    a = (a * a).astype(jnp.bfloat16)
    w2 = w2_ref[...].astype(jnp.bfloat16)
    for n in range(0, D_MODEL, MLP_TN):
        gate = mod_ref[0, :, 2 * D_MODEL + n:2 * D_MODEL + n + MLP_TN]
        o_ref[:, n:n + MLP_TN] += gate * jnp.dot(a, w2[:, n:n + MLP_TN], preferred_element_type=jnp.float32)

    if final_norm:
        @pl.when(j == pl.num_programs(1) - 1)
        def _():
            y = o_ref[...]
            inv = lax.rsqrt(jnp.mean(y * y, axis=-1, keepdims=True) + EPS)
            o_ref[...] = y * inv * fg_ref[...]


def _mlp_layer(x2d, mod, mod_idx, seq, norm_g, norm_idx, w1, w2, layer, final_g, final_norm):
    n_tok = x2d.shape[0]
    tiles_per_seq = seq // MLP_TM
    return pl.pallas_call(
        functools.partial(_mlp_kernel, final_norm),
        out_shape=jax.ShapeDtypeStruct(x2d.shape, jnp.float32),
        grid=(n_tok // MLP_TM, D_FF // MLP_TF),
        in_specs=[
            pl.BlockSpec((MLP_TM, D_MODEL), lambda i, j: (i, 0), pipeline_mode=pl.Buffered(1)),
            pl.BlockSpec((1, 1, 3 * D_MODEL), lambda i, j: (mod_idx + i // tiles_per_seq, 0, 0)),
            pl.BlockSpec((1, 1, D_MODEL), lambda i, j: (norm_idx, 0, 0)),
            pl.BlockSpec((None, D_MODEL, MLP_TF), lambda i, j: (layer, 0, j)),
            pl.BlockSpec((None, MLP_TF, D_MODEL), lambda i, j: (layer, j, 0)),
            pl.BlockSpec((1, D_MODEL), lambda i, j: (0, 0)),
        ],
        out_specs=pl.BlockSpec((MLP_TM, D_MODEL), lambda i, j: (i, 0)),
        scratch_shapes=[pltpu.VMEM((MLP_TM, D_MODEL), jnp.bfloat16)],
        compiler_params=_params("parallel", "arbitrary"),
        name="mlp_final" if final_norm else "mlp",
    )(x2d, mod, norm_g, w1, w2, final_g)


def _proj_kernel(x_ref, mod_ref, g_ref, wqk_ref, wvg_ref, cos_ref, sin_ref, o_ref, h_ref):
    j = pl.program_id(1)
    n_q = RET_QK_DIM // PROJ_TN
    half = RET_HEAD_QK // 2

    @pl.when(j == 0)
    def _():
        h_ref[...] = _norm_modulate(x_ref[...], g_ref[0], mod_ref[0]).astype(jnp.bfloat16)

    @pl.when(j < 2 * n_q)
    def _():
        r = jnp.dot(h_ref[...], wqk_ref[...], preferred_element_type=jnp.float32)
        k_scale = jnp.where(j >= n_q, RET_HEAD_QK ** -0.5, 1.0)
        cos = cos_ref[...] * k_scale
        sin = sin_ref[...] * k_scale
        for hd in range(PROJ_TN // RET_HEAD_QK):
            lo = hd * RET_HEAD_QK
            x1 = r[:, lo:lo + half]
            x2 = r[:, lo + half:lo + 2 * half]
            o_ref[:, lo:lo + half] = (x1 * cos - x2 * sin).astype(o_ref.dtype)
            o_ref[:, lo + half:lo + 2 * half] = (x1 * sin + x2 * cos).astype(o_ref.dtype)

    @pl.when(j >= 2 * n_q)
    def _():
        w = wvg_ref[...].astype(jnp.bfloat16)
        o_ref[...] = jnp.dot(h_ref[...], w, preferred_element_type=jnp.float32).astype(o_ref.dtype)


def _ret_projection(x2d, mod, mod_idx, seq, norm_g, norm_idx, w_qk, w_in, layer, cos, sin):
    n_tok = x2d.shape[0]
    tiles_per_seq = seq // PROJ_TM
    n_qk = 2 * RET_QK_DIM // PROJ_TN
    return pl.pallas_call(
        _proj_kernel,
        out_shape=jax.ShapeDtypeStruct((n_tok, RET_PROJ), jnp.bfloat16),
        grid=(n_tok // PROJ_TM, RET_PROJ // PROJ_TN),
        in_specs=[
            pl.BlockSpec((PROJ_TM, D_MODEL), lambda i, j: (i, 0), pipeline_mode=pl.Buffered(1)),
            pl.BlockSpec((1, 1, 3 * D_MODEL), lambda i, j: (mod_idx + i // tiles_per_seq, 0, 0)),
            pl.BlockSpec((1, 1, D_MODEL), lambda i, j: (norm_idx, 0, 0)),
            pl.BlockSpec((D_MODEL, PROJ_TN), lambda i, j: (0, jnp.minimum(j, n_qk - 1))),
            pl.BlockSpec((None, D_MODEL, PROJ_TN), lambda i, j: (layer, 0, jnp.maximum(j, n_qk))),
            pl.BlockSpec((PROJ_TM, RET_HEAD_QK // 2), lambda i, j: (i % tiles_per_seq, 0)),
            pl.BlockSpec((PROJ_TM, RET_HEAD_QK // 2), lambda i, j: (i % tiles_per_seq, 0)),
        ],
        out_specs=pl.BlockSpec((PROJ_TM, PROJ_TN), lambda i, j: (i, j)),
        scratch_shapes=[pltpu.VMEM((PROJ_TM, D_MODEL), jnp.bfloat16)],
        compiler_params=_params("parallel", "arbitrary"),
        name="ret_projection",
    )(x2d, mod, norm_g, w_qk, w_in, cos, sin)


def _scan_kernel(q_ref, k_ref, v_ref, g_ref, intra_ref, qdec_ref, kdec_ref, cdec_ref, o_ref, state_ref):
    @pl.when(pl.program_id(2) == 0)
    def _():
        state_ref[...] = jnp.zeros_like(state_ref)

    intra = intra_ref[...]
    q_dec = qdec_ref[...]
    k_dec = kdec_ref[...]
    chunk_dec = cdec_ref[...]
    for c in range(q_ref.shape[0] // RET_CHUNK):
        rows = slice(c * RET_CHUNK, (c + 1) * RET_CHUNK)
        qc = q_ref[rows, :]
        kc = k_ref[rows, :]
        vc = v_ref[rows, :]
        state = state_ref[...]
        scores = lax.dot_general(qc, kc, (((1,), (1,)), ((), ())), preferred_element_type=jnp.float32)
        scores = (scores * intra).astype(jnp.bfloat16)
        inner = jnp.dot(scores, vc, preferred_element_type=jnp.float32)
        cross = jnp.dot(qc, state.astype(jnp.bfloat16), preferred_element_type=jnp.float32) * q_dec
        k_decayed = (kc.astype(jnp.float32) * k_dec).astype(jnp.bfloat16)
        state_ref[...] = chunk_dec * state + lax.dot_general(
            k_decayed, vc, (((0,), (0,)), ((), ())), preferred_element_type=jnp.float32)
        y = inner + cross
        mu = jnp.mean(y, axis=-1, keepdims=True)
        yc = y - mu
        var = jnp.mean(yc * yc, axis=-1, keepdims=True)
        yn = yc * lax.rsqrt(var + EPS)
        o_ref[rows, :] = (jax.nn.silu(g_ref[rows, :].astype(jnp.float32)) * yn).astype(o_ref.dtype)


def _ret_scan(proj, batch, seq, intra, q_dec, k_dec, chunk_dec):
    n_tok = proj.shape[0]
    steps = seq // RET_TS
    k_off = RET_QK_DIM // RET_HEAD_QK
    v_off = 2 * RET_QK_DIM // RET_HEAD_V
    g_off = v_off + RET_HEADS
    return pl.pallas_call(
        _scan_kernel,
        out_shape=jax.ShapeDtypeStruct((n_tok, RET_V_DIM), jnp.bfloat16),
        grid=(batch, RET_HEADS, steps),
        in_specs=[
            pl.BlockSpec((RET_TS, RET_HEAD_QK), lambda b, h, t: (b * steps + t, h)),
            pl.BlockSpec((RET_TS, RET_HEAD_QK), lambda b, h, t: (b * steps + t, k_off + h)),
            pl.BlockSpec((RET_TS, RET_HEAD_V), lambda b, h, t: (b * steps + t, v_off + h)),
            pl.BlockSpec((RET_TS, RET_HEAD_V), lambda b, h, t: (b * steps + t, g_off + h)),
            pl.BlockSpec((None, RET_CHUNK, RET_CHUNK), lambda b, h, t: (h, 0, 0)),
            pl.BlockSpec((None, RET_CHUNK, 1), lambda b, h, t: (h, 0, 0)),
            pl.BlockSpec((None, RET_CHUNK, 1), lambda b, h, t: (h, 0, 0)),
            pl.BlockSpec((None, 1, 1), lambda b, h, t: (h, 0, 0)),
        ],
        out_specs=pl.BlockSpec((RET_TS, RET_HEAD_V), lambda b, h, t: (b * steps + t, h)),
        scratch_shapes=[pltpu.VMEM((RET_HEAD_QK, RET_HEAD_V), jnp.float32)],
        compiler_params=_params("parallel", "parallel", "arbitrary"),
        name="ret_scan",
    )(proj, proj, proj, proj, intra, q_dec, k_dec, chunk_dec)


def _out_kernel(z_ref, w_ref, x_ref, gate_ref, o_ref, wb_ref):
    @pl.when(pl.program_id(1) == 0)
    def _():
        wb_ref[...] = w_ref[...].astype(jnp.bfloat16)

    y = jnp.dot(z_ref[...], wb_ref[...], preferred_element_type=jnp.float32)
    o_ref[...] = x_ref[...] + gate_ref[0] * y


def _ret_output(z, x2d, mod, mod_idx, seq, w_out, layer):
    n_tok = x2d.shape[0]
    tiles_per_seq = seq // OUT_TM
    gate_off = 2 * D_MODEL // OUT_TN
    return pl.pallas_call(
        _out_kernel,
        out_shape=jax.ShapeDtypeStruct(x2d.shape, jnp.float32),
        grid=(D_MODEL // OUT_TN, n_tok // OUT_TM),
        in_specs=[
            pl.BlockSpec((OUT_TM, RET_V_DIM), lambda j, i: (i, 0)),
            pl.BlockSpec((None, RET_V_DIM, OUT_TN), lambda j, i: (layer, 0, j)),
            pl.BlockSpec((OUT_TM, OUT_TN), lambda j, i: (i, j)),
            pl.BlockSpec((1, 1, OUT_TN), lambda j, i: (mod_idx + i // tiles_per_seq, 0, gate_off + j)),
        ],
        out_specs=pl.BlockSpec((OUT_TM, OUT_TN), lambda j, i: (i, j)),
        scratch_shapes=[pltpu.VMEM((RET_V_DIM, OUT_TN), jnp.bfloat16)],
        compiler_params=_params("parallel", "arbitrary"),
        name="ret_output",
    )(z, w_out, x2d, mod)


def _rotary_tables(seq):
    half = RET_HEAD_QK // 2
    inv = ROPE_BASE ** (-jnp.arange(0, RET_HEAD_QK, 2, dtype=jnp.float32) / RET_HEAD_QK)
    ang = jnp.arange(seq, dtype=jnp.float32)[:, None] * inv[None, :]
    assert ang.shape == (seq, half)
    return jnp.cos(ang), jnp.sin(ang)


def _decay_tables():
    gamma = 1.0 - 2.0 ** (-5.0 - jnp.arange(RET_HEADS, dtype=jnp.float32))
    log_g = jnp.log(gamma)
    j = jnp.arange(RET_CHUNK, dtype=jnp.float32)
    diff = j[:, None] - j[None, :]
    intra = jnp.where(diff[None] >= 0, jnp.exp(jnp.maximum(diff, 0.0)[None] * log_g[:, None, None]), 0.0)
    q_dec = jnp.exp((j[None, :] + 1.0) * log_g[:, None])[:, :, None]
    k_dec = jnp.exp((RET_CHUNK - 1.0 - j[None, :]) * log_g[:, None])[:, :, None]
    chunk_dec = jnp.exp(RET_CHUNK * log_g)[:, None, None]
    return intra, q_dec, k_dec, chunk_dec


def _pair_split_qk(w_in):
    d_in = w_in.shape[0]
    n_qk = 2 * RET_QK_DIM
    qk = w_in[:, :n_qk].reshape(d_in, n_qk // RET_HEAD_QK, RET_HEAD_QK // 2, 2)
    return qk.transpose(0, 1, 3, 2).reshape(d_in, n_qk).astype(jnp.bfloat16)


@jax.jit
def kernel(x, c, ada_w, ada_b, norm_g, pool_w, pool_scale, ret_w_in, ret_w_out, mlp_w1, mlp_w2, final_g):
    batch, seq, d = x.shape
    assert d == D_MODEL and batch <= 8
    assert seq % max(POOL_TM, MLP_TM, PROJ_TM, OUT_TM, RET_TS) == 0

    c_pad = jnp.zeros((8, d), jnp.float32).at[:batch].set(c)
    mod = _ada_modulation(c_pad, ada_w.reshape(DEPTH * 2, d, 3 * d), ada_b.reshape(DEPTH * 2, 1, 3 * d))
    mod = mod[:, :batch].reshape(DEPTH * 2 * batch, 1, 3 * d)

    cos, sin = _rotary_tables(seq)
    intra, q_dec, k_dec, chunk_dec = _decay_tables()
    norm_rows = norm_g.reshape(DEPTH * 2, 1, d)
    scale_rows = pool_scale.reshape(-1, 1, d)
    final_row = final_g.reshape(1, d)

    x2d = x.reshape(batch * seq, d)
    for i in range(DEPTH):
        mix_idx = (2 * i) * batch
        mlp_idx = (2 * i + 1) * batch
        if i % 2 == 0:
            x2d = _pool_layer(x2d, mod, mix_idx, seq, norm_rows, 2 * i, pool_w, scale_rows, i // 2)
        else:
            w_qk = _pair_split_qk(ret_w_in[i // 2])
            proj = _ret_projection(x2d, mod, mix_idx, seq, norm_rows, 2 * i, w_qk, ret_w_in, i // 2, cos, sin)
            z = _ret_scan(proj, batch, seq, intra, q_dec, k_dec, chunk_dec)
            x2d = _ret_output(z, x2d, mod, mix_idx, seq, ret_w_out, i // 2)
        x2d = _mlp_layer(x2d, mod, mlp_idx, seq, norm_rows, 2 * i + 1, mlp_w1, mlp_w2, i, final_row,
                         final_norm=(i == DEPTH - 1))
    return x2d.reshape(batch, seq, d)
```

```python
import functools

import jax
import jax.numpy as jnp
from jax import lax
from jax.experimental import pallas as pl
from jax.experimental.pallas import tpu as pltpu

D_MODEL = 2048
DEPTH = 4
POOL_WINDOWS = (2, 4, 8, 16)
POOL_GROUP = D_MODEL // len(POOL_WINDOWS)
POOL_HALO = 16
RET_HEADS = 8
RET_QK_DIM = D_MODEL
RET_V_DIM = 2 * D_MODEL
RET_HEAD_QK = RET_QK_DIM // RET_HEADS
RET_HEAD_V = RET_V_DIM // RET_HEADS
RET_PROJ = 2 * RET_QK_DIM + 2 * RET_V_DIM
ROPE_BASE = 10000.0
D_FF = 4 * D_MODEL
EPS = 1e-6
LANES = 128

V7X_VMEM_BYTES = 64 * 1024 * 1024
VMEM_LIMIT_BYTES = V7X_VMEM_BYTES - 4 * 1024 * 1024

ADA_TK = 1024
POOL_TM = 512
MLP_TM = 1024
MLP_TF = 512
MLP_TN = 512
PROJ_TM = 1024
PROJ_TN = 1024
RET_TS = 1024
RET_CHUNK = 256
OUT_TM = 1024
OUT_TN = 512


def _params(*semantics):
    return pltpu.CompilerParams(dimension_semantics=semantics, vmem_limit_bytes=VMEM_LIMIT_BYTES)


def _norm_modulate(x, g, mod_row):
    shift = mod_row[:, 0:D_MODEL]
    scale = mod_row[:, D_MODEL:2 * D_MODEL]
    inv = lax.rsqrt(jnp.mean(x * x, axis=-1, keepdims=True) + EPS)
    return x * inv * (g * (1.0 + scale)) + shift


def _ada_kernel(c_ref, w_ref, b_ref, o_ref):
    cond = jax.nn.silu(c_ref[...]).astype(jnp.bfloat16)
    w = w_ref[...].astype(jnp.bfloat16)
    o_ref[...] = jnp.dot(cond, w, preferred_element_type=jnp.float32) + b_ref[...]


def _ada_modulation(c_pad, ada_w, ada_b):
    n_ls = ada_w.shape[0]
    n_out = ada_w.shape[2]
    return pl.pallas_call(
        _ada_kernel,
        out_shape=jax.ShapeDtypeStruct((n_ls, 8, n_out), jnp.float32),
        grid=(n_ls, n_out // ADA_TK),
        in_specs=[
            pl.BlockSpec((8, D_MODEL), lambda l, j: (0, 0)),
            pl.BlockSpec((None, D_MODEL, ADA_TK), lambda l, j: (l, 0, j)),
            pl.BlockSpec((None, 1, ADA_TK), lambda l, j: (l, 0, j)),
        ],
        out_specs=pl.BlockSpec((None, 8, ADA_TK), lambda l, j: (l, 0, j)),
        compiler_params=_params("parallel", "parallel"),
        name="ada_modulation",
    )(c_pad, ada_w, ada_b)


def _pool_kernel(tiles_per_seq, x_ref, halo_ref, mod_ref, g_ref, w_ref, ls_ref, o_ref, wb_ref):
    i = pl.program_id(0)

    @pl.when(i == 0)
    def _():
        wb_ref[...] = w_ref[...].astype(jnp.bfloat16)

    tile_in_seq = i % tiles_per_seq
    tm = x_ref.shape[0]
    mod_row = mod_ref[0]
    g = g_ref[0]
    x = x_ref[...]
    h = _norm_modulate(x, g, mod_row)
    h_halo = _norm_modulate(halo_ref[...], g, mod_row)
    h_halo = jnp.where(tile_in_seq == 0, 0.0, h_halo)
    gate = mod_row[:, 2 * D_MODEL:3 * D_MODEL]
    ls = ls_ref[0]
    pos = tile_in_seq * tm + lax.broadcasted_iota(jnp.int32, (tm, 1), 0)
    for gi, window in enumerate(POOL_WINDOWS):
        cols = slice(gi * POOL_GROUP, (gi + 1) * POOL_GROUP)
        hg = h[:, cols]
        acc = jnp.concatenate([h_halo[:, cols], hg], axis=0)
        shift = 1
        while shift < window:
            acc = acc + pltpu.roll(acc, shift, axis=0)
            shift *= 2
        cnt = jnp.minimum(pos + 1, window).astype(jnp.float32)
        p = acc[POOL_HALO:, :] / cnt - hg
        y = jnp.dot(p.astype(jnp.bfloat16), wb_ref[gi], preferred_element_type=jnp.float32)
        o_ref[:, cols] = x[:, cols] + gate[:, cols] * (y * ls[:, cols])


def _pool_layer(x2d, mod, mod_idx, seq, norm_g, norm_idx, pool_w, pool_scale, pool_idx):
    n_tok = x2d.shape[0]
    tiles_per_seq = seq // POOL_TM
    halo_blocks_per_tile = POOL_TM // POOL_HALO
    n_grp = len(POOL_WINDOWS)
    return pl.pallas_call(
        functools.partial(_pool_kernel, tiles_per_seq),
        out_shape=jax.ShapeDtypeStruct(x2d.shape, jnp.float32),
        grid=(n_tok // POOL_TM,),
        in_specs=[
            pl.BlockSpec((POOL_TM, D_MODEL), lambda i: (i, 0)),
            pl.BlockSpec((POOL_HALO, D_MODEL), lambda i: (jnp.maximum(i * halo_blocks_per_tile - 1, 0), 0)),
            pl.BlockSpec((1, 1, 3 * D_MODEL), lambda i: (mod_idx + i // tiles_per_seq, 0, 0)),
            pl.BlockSpec((1, 1, D_MODEL), lambda i: (norm_idx, 0, 0)),
            pl.BlockSpec((None, n_grp, POOL_GROUP, POOL_GROUP), lambda i: (pool_idx, 0, 0, 0)),
            pl.BlockSpec((1, 1, D_MODEL), lambda i: (pool_idx, 0, 0)),
        ],
        out_specs=pl.BlockSpec((POOL_TM, D_MODEL), lambda i: (i, 0)),
        scratch_shapes=[pltpu.VMEM((n_grp, POOL_GROUP, POOL_GROUP), jnp.bfloat16)],
        compiler_params=_params("arbitrary"),
        name="pool_mixer",
    )(x2d, x2d, mod, norm_g, pool_w, pool_scale)


def _mlp_kernel(final_norm, x_ref, mod_ref, g_ref, w1_ref, w2_ref, fg_ref, o_ref, h_ref):
    j = pl.program_id(1)

    @pl.when(j == 0)
    def _():
        x = x_ref[...]
        h_ref[...] = _norm_modulate(x, g_ref[0], mod_ref[0]).astype(jnp.bfloat16)
        o_ref[...] = x

    a = jnp.dot(h_ref[...], w1_ref[...].astype(jnp.bfloat16), preferred_element_type=jnp.float32)
    a = jnp.maximum(a, 0.0)
    a = (a * a).astype(jnp.bfloat16)
    w2 = w2_ref[...].astype(jnp.bfloat16)
    for n in range(0, D_MODEL, MLP_TN):
        gate = mod_ref[0, :, 2 * D_MODEL + n:2 * D_MODEL + n + MLP_TN]
        o_ref[:, n:n + MLP_TN] += gate * jnp.dot(a, w2[:, n:n + MLP_TN], preferred_element_type=jnp.float32)

    if final_norm:
        @pl.when(j == pl.num_programs(1) - 1)
        def _():
            y = o_ref[...]
            inv = lax.rsqrt(jnp.mean(y * y, axis=-1, keepdims=True) + EPS)
            o_ref[...] = y * inv * fg_ref[...]


def _mlp_layer(x2d, mod, mod_idx, seq, norm_g, norm_idx, w1, w2, layer, final_g, final_norm):
    n_tok = x2d.shape[0]
    tiles_per_seq = seq // MLP_TM
    return pl.pallas_call(
        functools.partial(_mlp_kernel, final_norm),
        out_shape=jax.ShapeDtypeStruct(x2d.shape, jnp.float32),
        grid=(n_tok // MLP_TM, D_FF // MLP_TF),
        in_specs=[
            pl.BlockSpec((MLP_TM, D_MODEL), lambda i, j: (i, 0), pipeline_mode=pl.Buffered(1)),
            pl.BlockSpec((1, 1, 3 * D_MODEL), lambda i, j: (mod_idx + i // tiles_per_seq, 0, 0)),
            pl.BlockSpec((1, 1, D_MODEL), lambda i, j: (norm_idx, 0, 0)),
            pl.BlockSpec((None, D_MODEL, MLP_TF), lambda i, j: (layer, 0, j)),
            pl.BlockSpec((None, MLP_TF, D_MODEL), lambda i, j: (layer, j, 0)),
            pl.BlockSpec((1, D_MODEL), lambda i, j: (0, 0)),
        ],
        out_specs=pl.BlockSpec((MLP_TM, D_MODEL), lambda i, j: (i, 0)),
        scratch_shapes=[pltpu.VMEM((MLP_TM, D_MODEL), jnp.bfloat16)],
        compiler_params=_params("parallel", "arbitrary"),
        name="mlp_final" if final_norm else "mlp",
    )(x2d, mod, norm_g, w1, w2, final_g)


def _proj_kernel(x_ref, mod_ref, g_ref, w_ref, cos_ref, sin_ref, o_ref, h_ref):
    j = pl.program_id(1)
    n_q = RET_QK_DIM // PROJ_TN

    @pl.when(j == 0)
    def _():
        h_ref[...] = _norm_modulate(x_ref[...], g_ref[0], mod_ref[0]).astype(jnp.bfloat16)

    @pl.when(j < 2 * n_q)
    def _():
        r = jnp.dot(h_ref[...], w_ref[...].astype(jnp.bfloat16), preferred_element_type=jnp.float32)
        k_scale = jnp.where(j >= n_q, RET_HEAD_QK ** -0.5, 1.0)
        cos = cos_ref[...] * k_scale
        sin = sin_ref[...] * k_scale
        even = lax.broadcasted_iota(jnp.int32, (1, LANES), 1) % 2 == 0
        for lo in range(0, PROJ_TN, LANES):
            t = lo % RET_HEAD_QK
            xs = r[:, lo:lo + LANES]
            partner = jnp.where(even, pltpu.roll(xs, LANES - 1, axis=1), pltpu.roll(xs, 1, axis=1))
            o_ref[:, lo:lo + LANES] = (xs * cos[:, t:t + LANES] + partner * sin[:, t:t + LANES]).astype(o_ref.dtype)

    @pl.when(j >= 2 * n_q)
    def _():
        w = w_ref[...].astype(jnp.bfloat16)
        o_ref[...] = jnp.dot(h_ref[...], w, preferred_element_type=jnp.float32).astype(o_ref.dtype)


def _ret_projection(x2d, mod, mod_idx, seq, norm_g, norm_idx, w_in, layer, cos, sin):
    n_tok = x2d.shape[0]
    tiles_per_seq = seq // PROJ_TM
    return pl.pallas_call(
        _proj_kernel,
        out_shape=jax.ShapeDtypeStruct((n_tok, RET_PROJ), jnp.bfloat16),
        grid=(n_tok // PROJ_TM, RET_PROJ // PROJ_TN),
        in_specs=[
            pl.BlockSpec((PROJ_TM, D_MODEL), lambda i, j: (i, 0), pipeline_mode=pl.Buffered(1)),
            pl.BlockSpec((1, 1, 3 * D_MODEL), lambda i, j: (mod_idx + i // tiles_per_seq, 0, 0)),
            pl.BlockSpec((1, 1, D_MODEL), lambda i, j: (norm_idx, 0, 0)),
            pl.BlockSpec((None, D_MODEL, PROJ_TN), lambda i, j: (layer, 0, j)),
            pl.BlockSpec((PROJ_TM, RET_HEAD_QK), lambda i, j: (i % tiles_per_seq, 0)),
            pl.BlockSpec((PROJ_TM, RET_HEAD_QK), lambda i, j: (i % tiles_per_seq, 0)),
        ],
        out_specs=pl.BlockSpec((PROJ_TM, PROJ_TN), lambda i, j: (i, j)),
        scratch_shapes=[pltpu.VMEM((PROJ_TM, D_MODEL), jnp.bfloat16)],
        compiler_params=_params("parallel", "arbitrary"),
        name="ret_projection",
    )(x2d, mod, norm_g, w_in, cos, sin)


def _scan_kernel(q_ref, k_ref, v_ref, g_ref, intra_ref, qdec_ref, kdec_ref, cdec_ref, o_ref, state_ref):
    @pl.when(pl.program_id(2) == 0)
    def _():
        state_ref[...] = jnp.zeros_like(state_ref)

    intra = intra_ref[...]
    q_dec = qdec_ref[...]
    k_dec = kdec_ref[...]
    chunk_dec = cdec_ref[...]
    for c in range(q_ref.shape[0] // RET_CHUNK):
        rows = slice(c * RET_CHUNK, (c + 1) * RET_CHUNK)
        qc = q_ref[rows, :]
        kc = k_ref[rows, :]
        vc = v_ref[rows, :]
        state = state_ref[...]
        scores = lax.dot_general(qc, kc, (((1,), (1,)), ((), ())), preferred_element_type=jnp.float32)
        scores = (scores * intra).astype(jnp.bfloat16)
        inner = jnp.dot(scores, vc, preferred_element_type=jnp.float32)
        cross = jnp.dot(qc, state.astype(jnp.bfloat16), preferred_element_type=jnp.float32) * q_dec
        k_decayed = (kc.astype(jnp.float32) * k_dec).astype(jnp.bfloat16)
        state_ref[...] = chunk_dec * state + lax.dot_general(
            k_decayed, vc, (((0,), (0,)), ((), ())), preferred_element_type=jnp.float32)
        y = inner + cross
        mu = jnp.mean(y, axis=-1, keepdims=True)
        yc = y - mu
        var = jnp.mean(yc * yc, axis=-1, keepdims=True)
        yn = yc * lax.rsqrt(var + EPS)
        o_ref[rows, :] = (jax.nn.silu(g_ref[rows, :].astype(jnp.float32)) * yn).astype(o_ref.dtype)


def _ret_scan(proj, batch, seq, intra, q_dec, k_dec, chunk_dec):
    n_tok = proj.shape[0]
    steps = seq // RET_TS
    k_off = RET_QK_DIM // RET_HEAD_QK
    v_off = 2 * RET_QK_DIM // RET_HEAD_V
    g_off = v_off + RET_HEADS
    return pl.pallas_call(
        _scan_kernel,
        out_shape=jax.ShapeDtypeStruct((n_tok, RET_V_DIM), jnp.bfloat16),
        grid=(batch, RET_HEADS, steps),
        in_specs=[
            pl.BlockSpec((RET_TS, RET_HEAD_QK), lambda b, h, t: (b * steps + t, h)),
            pl.BlockSpec((RET_TS, RET_HEAD_QK), lambda b, h, t: (b * steps + t, k_off + h)),
            pl.BlockSpec((RET_TS, RET_HEAD_V), lambda b, h, t: (b * steps + t, v_off + h)),
            pl.BlockSpec((RET_TS, RET_HEAD_V), lambda b, h, t: (b * steps + t, g_off + h)),
            pl.BlockSpec((None, RET_CHUNK, RET_CHUNK), lambda b, h, t: (h, 0, 0)),
            pl.BlockSpec((None, RET_CHUNK, 1), lambda b, h, t: (h, 0, 0)),
            pl.BlockSpec((None, RET_CHUNK, 1), lambda b, h, t: (h, 0, 0)),
            pl.BlockSpec((None, 1, 1), lambda b, h, t: (h, 0, 0)),
        ],
        out_specs=pl.BlockSpec((RET_TS, RET_HEAD_V), lambda b, h, t: (b * steps + t, h)),
        scratch_shapes=[pltpu.VMEM((RET_HEAD_QK, RET_HEAD_V), jnp.float32)],
        compiler_params=_params("parallel", "parallel", "arbitrary"),
        name="ret_scan",
    )(proj, proj, proj, proj, intra, q_dec, k_dec, chunk_dec)


def _out_kernel(z_ref, w_ref, x_ref, gate_ref, o_ref, wb_ref):
    @pl.when(pl.program_id(1) == 0)
    def _():
        wb_ref[...] = w_ref[...].astype(jnp.bfloat16)

    y = jnp.dot(z_ref[...], wb_ref[...], preferred_element_type=jnp.float32)
    o_ref[...] = x_ref[...] + gate_ref[0] * y


def _ret_output(z, x2d, mod, mod_idx, seq, w_out, layer):
    n_tok = x2d.shape[0]
    tiles_per_seq = seq // OUT_TM
    gate_off = 2 * D_MODEL // OUT_TN
    return pl.pallas_call(
        _out_kernel,
        out_shape=jax.ShapeDtypeStruct(x2d.shape, jnp.float32),
        grid=(D_MODEL // OUT_TN, n_tok // OUT_TM),
        in_specs=[
            pl.BlockSpec((OUT_TM, RET_V_DIM), lambda j, i: (i, 0)),
            pl.BlockSpec((None, RET_V_DIM, OUT_TN), lambda j, i: (layer, 0, j)),
            pl.BlockSpec((OUT_TM, OUT_TN), lambda j, i: (i, j)),
            pl.BlockSpec((1, 1, OUT_TN), lambda j, i: (mod_idx + i // tiles_per_seq, 0, gate_off + j)),
        ],
        out_specs=pl.BlockSpec((OUT_TM, OUT_TN), lambda j, i: (i, j)),
        scratch_shapes=[pltpu.VMEM((RET_V_DIM, OUT_TN), jnp.bfloat16)],
        compiler_params=_params("parallel", "arbitrary"),
        name="ret_output",
    )(z, w_out, x2d, mod)


def _rotary_tables(seq):
    inv = ROPE_BASE ** (-jnp.arange(0, RET_HEAD_QK, 2, dtype=jnp.float32) / RET_HEAD_QK)
    ang = jnp.arange(seq, dtype=jnp.float32)[:, None] * inv[None, :]
    cos = jnp.repeat(jnp.cos(ang), 2, axis=1)
    sin = jnp.stack([-jnp.sin(ang), jnp.sin(ang)], axis=-1).reshape(seq, RET_HEAD_QK)
    return cos, sin


def _decay_tables():
    gamma = 1.0 - 2.0 ** (-5.0 - jnp.arange(RET_HEADS, dtype=jnp.float32))
    log_g = jnp.log(gamma)
    j = jnp.arange(RET_CHUNK, dtype=jnp.float32)
    diff = j[:, None] - j[None, :]
    intra = jnp.where(diff[None] >= 0, jnp.exp(jnp.maximum(diff, 0.0)[None] * log_g[:, None, None]), 0.0)
    q_dec = jnp.exp((j[None, :] + 1.0) * log_g[:, None])[:, :, None]
    k_dec = jnp.exp((RET_CHUNK - 1.0 - j[None, :]) * log_g[:, None])[:, :, None]
    chunk_dec = jnp.exp(RET_CHUNK * log_g)[:, None, None]
    return intra, q_dec, k_dec, chunk_dec


@jax.jit
def kernel(x, c, ada_w, ada_b, norm_g, pool_w, pool_scale, ret_w_in, ret_w_out, mlp_w1, mlp_w2, final_g):
    batch, seq, d = x.shape
    assert d == D_MODEL and batch <= 8
    assert seq % max(POOL_TM, MLP_TM, PROJ_TM, OUT_TM, RET_TS) == 0

    c_pad = jnp.zeros((8, d), jnp.float32).at[:batch].set(c)
    mod = _ada_modulation(c_pad, ada_w.reshape(DEPTH * 2, d, 3 * d), ada_b.reshape(DEPTH * 2, 1, 3 * d))
    mod = mod[:, :batch].reshape(DEPTH * 2 * batch, 1, 3 * d)

    cos, sin = _rotary_tables(seq)
    intra, q_dec, k_dec, chunk_dec = _decay_tables()
    norm_rows = norm_g.reshape(DEPTH * 2, 1, d)
    scale_rows = pool_scale.reshape(-1, 1, d)
    final_row = final_g.reshape(1, d)

    x2d = x.reshape(batch * seq, d)
    for i in range(DEPTH):
        mix_idx = (2 * i) * batch
        mlp_idx = (2 * i + 1) * batch
        if i % 2 == 0:
            x2d = _pool_layer(x2d, mod, mix_idx, seq, norm_rows, 2 * i, pool_w, scale_rows, i // 2)
        else:
            proj = _ret_projection(x2d, mod, mix_idx, seq, norm_rows, 2 * i, ret_w_in, i // 2, cos, sin)
            z = _ret_scan(proj, batch, seq, intra, q_dec, k_dec, chunk_dec)
            x2d = _ret_output(z, x2d, mod, mix_idx, seq, ret_w_out, i // 2)
        x2d = _mlp_layer(x2d, mod, mlp_idx, seq, norm_rows, 2 * i + 1, mlp_w1, mlp_w2, i, final_row,
                         final_norm=(i == DEPTH - 1))
    return x2d.reshape(batch, seq, d)
```

```python
import functools

import jax
import jax.numpy as jnp
import numpy as np
from jax import lax
from jax.experimental import pallas as pl
from jax.experimental.pallas import tpu as pltpu

D_MODEL = 2048
DEPTH = 4
POOL_WINDOWS = (2, 4, 8, 16)
POOL_GROUP = D_MODEL // len(POOL_WINDOWS)
POOL_HALO = 16
RET_HEADS = 8
RET_QK_DIM = D_MODEL
RET_V_DIM = 2 * D_MODEL
RET_HEAD_QK = RET_QK_DIM // RET_HEADS
RET_HEAD_V = RET_V_DIM // RET_HEADS
RET_PROJ = 2 * RET_QK_DIM + 2 * RET_V_DIM
ROPE_BASE = 10000.0
D_FF = 4 * D_MODEL
EPS = 1e-6
LANES = 128

V7X_VMEM_BYTES = 64 * 1024 * 1024
VMEM_LIMIT_BYTES = V7X_VMEM_BYTES - 4 * 1024 * 1024

ROW_CHUNK = 128
ADA_TK = 1024
POOL_TM = 512
MLP_TM = 1024
MLP_TF = 512
MLP_TN = 512
PROJ_TM = 1024
PROJ_TN = 1024
RET_TS = 1024
RET_CHUNK = 256
OUT_TM = 1024
OUT_TN = 512


def _params(*semantics):
    return pltpu.CompilerParams(dimension_semantics=semantics, vmem_limit_bytes=VMEM_LIMIT_BYTES)


def _norm_modulate(x, g, mod_row):
    shift = mod_row[:, 0:D_MODEL]
    scale = mod_row[:, D_MODEL:2 * D_MODEL]
    inv = lax.rsqrt(jnp.mean(x * x, axis=-1, keepdims=True) + EPS)
    return x * inv * (g * (1.0 + scale)) + shift


def _ada_kernel(c_ref, w_ref, b_ref, o_ref):
    cond = jax.nn.silu(c_ref[...]).astype(jnp.bfloat16)
    w = w_ref[...].astype(jnp.bfloat16)
    o_ref[...] = jnp.dot(cond, w, preferred_element_type=jnp.float32) + b_ref[...]


def _ada_modulation(c_pad, ada_w, ada_b):
    n_ls = ada_w.shape[0]
    n_out = ada_w.shape[2]
    return pl.pallas_call(
        _ada_kernel,
        out_shape=jax.ShapeDtypeStruct((n_ls, 8, n_out), jnp.float32),
        grid=(n_ls, n_out // ADA_TK),
        in_specs=[
            pl.BlockSpec((8, D_MODEL), lambda l, j: (0, 0)),
            pl.BlockSpec((None, D_MODEL, ADA_TK), lambda l, j: (l, 0, j)),
            pl.BlockSpec((None, 1, ADA_TK), lambda l, j: (l, 0, j)),
        ],
        out_specs=pl.BlockSpec((None, 8, ADA_TK), lambda l, j: (l, 0, j)),
        compiler_params=_params("parallel", "parallel"),
        name="ada_modulation",
    )(c_pad, ada_w, ada_b)


def _pool_kernel(tiles_per_seq, x_ref, halo_ref, mod_ref, g_ref, w_ref, ls_ref, o_ref, wb_ref):
    i = pl.program_id(0)

    @pl.when(i == 0)
    def _():
        wb_ref[...] = w_ref[...].astype(jnp.bfloat16)

    tile_in_seq = i % tiles_per_seq
    tm = x_ref.shape[0]
    mod_row = mod_ref[0]
    g = g_ref[0]
    x = x_ref[...]
    h = _norm_modulate(x, g, mod_row)
    h_halo = _norm_modulate(halo_ref[...], g, mod_row)
    h_halo = jnp.where(tile_in_seq == 0, 0.0, h_halo)
    gate = mod_row[:, 2 * D_MODEL:3 * D_MODEL]
    ls = ls_ref[0]
    pos = tile_in_seq * tm + lax.broadcasted_iota(jnp.int32, (tm, 1), 0)
    for gi, window in enumerate(POOL_WINDOWS):
        cols = slice(gi * POOL_GROUP, (gi + 1) * POOL_GROUP)
        hg = h[:, cols]
        acc = jnp.concatenate([h_halo[:, cols], hg], axis=0)
        shift = 1
        while shift < window:
            acc = acc + pltpu.roll(acc, shift, axis=0)
            shift *= 2
        cnt = jnp.minimum(pos + 1, window).astype(jnp.float32)
        p = acc[POOL_HALO:, :] / cnt - hg
        y = jnp.dot(p.astype(jnp.bfloat16), wb_ref[gi], preferred_element_type=jnp.float32)
        o_ref[:, cols] = x[:, cols] + gate[:, cols] * (y * ls[:, cols])


def _pool_layer(x2d, mod, mod_idx, seq, norm_g, norm_idx, pool_w, pool_scale, pool_idx):
    n_tok = x2d.shape[0]
    tiles_per_seq = seq // POOL_TM
    halo_blocks_per_tile = POOL_TM // POOL_HALO
    n_grp = len(POOL_WINDOWS)
    return pl.pallas_call(
        functools.partial(_pool_kernel, tiles_per_seq),
        out_shape=jax.ShapeDtypeStruct(x2d.shape, jnp.float32),
        grid=(n_tok // POOL_TM,),
        in_specs=[
            pl.BlockSpec((POOL_TM, D_MODEL), lambda i: (i, 0)),
            pl.BlockSpec((POOL_HALO, D_MODEL), lambda i: (jnp.maximum(i * halo_blocks_per_tile - 1, 0), 0)),
            pl.BlockSpec((1, 1, 3 * D_MODEL), lambda i: (mod_idx + i // tiles_per_seq, 0, 0)),
            pl.BlockSpec((1, 1, D_MODEL), lambda i: (norm_idx, 0, 0)),
            pl.BlockSpec((None, n_grp, POOL_GROUP, POOL_GROUP), lambda i: (pool_idx, 0, 0, 0)),
            pl.BlockSpec((1, 1, D_MODEL), lambda i: (pool_idx, 0, 0)),
        ],
        out_specs=pl.BlockSpec((POOL_TM, D_MODEL), lambda i: (i, 0)),
        scratch_shapes=[pltpu.VMEM((n_grp, POOL_GROUP, POOL_GROUP), jnp.bfloat16)],
        compiler_params=_params("arbitrary"),
        name="pool_mixer",
    )(x2d, x2d, mod, norm_g, pool_w, pool_scale)


def _row_chunk_copies(x_hbm, first_row, dst_of_chunk, sems):
    return [
        pltpu.make_async_copy(x_hbm.at[pl.ds(first_row + k * ROW_CHUNK, ROW_CHUNK), :], dst_of_chunk(k), sems.at[k])
        for k in range(sems.shape[0])
    ]


def _mlp_kernel(final_norm, x_hbm, mod_ref, g_ref, w1_ref, w2_ref, fg_ref, o_ref, h_ref, sems):
    j = pl.program_id(1)

    @pl.when(j == 0)
    def _():
        first_row = pl.multiple_of(pl.program_id(0) * MLP_TM, MLP_TM)
        copies = _row_chunk_copies(x_hbm, first_row, lambda k: o_ref.at[pl.ds(k * ROW_CHUNK, ROW_CHUNK), :], sems)
        for cp in copies:
            cp.start()
        for k, cp in enumerate(copies):
            cp.wait()
            rows = pl.ds(k * ROW_CHUNK, ROW_CHUNK)
            h_ref[rows, :] = _norm_modulate(o_ref[rows, :], g_ref[0], mod_ref[0]).astype(jnp.bfloat16)

    a = jnp.dot(h_ref[...], w1_ref[...].astype(jnp.bfloat16), preferred_element_type=jnp.float32)
    a = jnp.maximum(a, 0.0)
    a = (a * a).astype(jnp.bfloat16)
    w2 = w2_ref[...].astype(jnp.bfloat16)
    for n in range(0, D_MODEL, MLP_TN):
        gate = mod_ref[0, :, 2 * D_MODEL + n:2 * D_MODEL + n + MLP_TN]
        o_ref[:, n:n + MLP_TN] += gate * jnp.dot(a, w2[:, n:n + MLP_TN], preferred_element_type=jnp.float32)

    if final_norm:
        @pl.when(j == pl.num_programs(1) - 1)
        def _():
            y = o_ref[...]
            inv = lax.rsqrt(jnp.mean(y * y, axis=-1, keepdims=True) + EPS)
            o_ref[...] = y * inv * fg_ref[...]


def _mlp_layer(x2d, mod, mod_idx, seq, norm_g, norm_idx, w1, w2, layer, final_g, final_norm):
    n_tok = x2d.shape[0]
    tiles_per_seq = seq // MLP_TM
    return pl.pallas_call(
        functools.partial(_mlp_kernel, final_norm),
        out_shape=jax.ShapeDtypeStruct(x2d.shape, jnp.float32),
        grid=(n_tok // MLP_TM, D_FF // MLP_TF),
        in_specs=[
            pl.BlockSpec(memory_space=pl.ANY),
            pl.BlockSpec((1, 1, 3 * D_MODEL), lambda i, j: (mod_idx + i // tiles_per_seq, 0, 0)),
            pl.BlockSpec((1, 1, D_MODEL), lambda i, j: (norm_idx, 0, 0)),
            pl.BlockSpec((None, D_MODEL, MLP_TF), lambda i, j: (layer, 0, j)),
            pl.BlockSpec((None, MLP_TF, D_MODEL), lambda i, j: (layer, j, 0)),
            pl.BlockSpec((1, D_MODEL), lambda i, j: (0, 0)),
        ],
        out_specs=pl.BlockSpec((MLP_TM, D_MODEL), lambda i, j: (i, 0)),
        scratch_shapes=[pltpu.VMEM((MLP_TM, D_MODEL), jnp.bfloat16),
                        pltpu.SemaphoreType.DMA((MLP_TM // ROW_CHUNK,))],
        compiler_params=_params("parallel", "arbitrary"),
        name="mlp_final" if final_norm else "mlp",
    )(x2d, mod, norm_g, w1, w2, final_g)


def _proj_kernel(x_hbm, mod_ref, g_ref, w_ref, cos_ref, sin_ref, o_ref, h_ref, xbuf_ref, sems):
    j = pl.program_id(1)
    n_q = RET_QK_DIM // PROJ_TN

    @pl.when(j == 0)
    def _():
        first_row = pl.multiple_of(pl.program_id(0) * PROJ_TM, PROJ_TM)
        copies = _row_chunk_copies(x_hbm, first_row, lambda k: xbuf_ref.at[k], sems)
        for cp in copies:
            cp.start()
        for k, cp in enumerate(copies):
            cp.wait()
            rows = pl.ds(k * ROW_CHUNK, ROW_CHUNK)
            h_ref[rows, :] = _norm_modulate(xbuf_ref[k], g_ref[0], mod_ref[0]).astype(jnp.bfloat16)

    @pl.when(j < 2 * n_q)
    def _():
        r = jnp.dot(h_ref[...], w_ref[...].astype(jnp.bfloat16), preferred_element_type=jnp.float32)
        k_scale = jnp.where(j >= n_q, RET_HEAD_QK ** -0.5, 1.0)
        cos = cos_ref[...] * k_scale
        sin = sin_ref[...] * k_scale
        even = lax.broadcasted_iota(jnp.int32, (1, LANES), 1) % 2 == 0
        for lo in range(0, PROJ_TN, LANES):
            t = lo % RET_HEAD_QK
            xs = r[:, lo:lo + LANES]
            partner = jnp.where(even, pltpu.roll(xs, LANES - 1, axis=1), pltpu.roll(xs, 1, axis=1))
            o_ref[:, lo:lo + LANES] = (xs * cos[:, t:t + LANES] + partner * sin[:, t:t + LANES]).astype(o_ref.dtype)

    @pl.when(j >= 2 * n_q)
    def _():
        w = w_ref[...].astype(jnp.bfloat16)
        o_ref[...] = jnp.dot(h_ref[...], w, preferred_element_type=jnp.float32).astype(o_ref.dtype)


def _ret_projection(x2d, mod, mod_idx, seq, norm_g, norm_idx, w_in, layer, cos, sin):
    n_tok = x2d.shape[0]
    tiles_per_seq = seq // PROJ_TM
    return pl.pallas_call(
        _proj_kernel,
        out_shape=jax.ShapeDtypeStruct((n_tok, RET_PROJ), jnp.bfloat16),
        grid=(n_tok // PROJ_TM, RET_PROJ // PROJ_TN),
        in_specs=[
            pl.BlockSpec(memory_space=pl.ANY),
            pl.BlockSpec((1, 1, 3 * D_MODEL), lambda i, j: (mod_idx + i // tiles_per_seq, 0, 0)),
            pl.BlockSpec((1, 1, D_MODEL), lambda i, j: (norm_idx, 0, 0)),
            pl.BlockSpec((None, D_MODEL, PROJ_TN), lambda i, j: (layer, 0, j)),
            pl.BlockSpec((PROJ_TM, RET_HEAD_QK), lambda i, j: (i % tiles_per_seq, 0)),
            pl.BlockSpec((PROJ_TM, RET_HEAD_QK), lambda i, j: (i % tiles_per_seq, 0)),
        ],
        out_specs=pl.BlockSpec((PROJ_TM, PROJ_TN), lambda i, j: (i, j)),
        scratch_shapes=[pltpu.VMEM((PROJ_TM, D_MODEL), jnp.bfloat16),
                        pltpu.VMEM((PROJ_TM // ROW_CHUNK, ROW_CHUNK, D_MODEL), jnp.float32),
                        pltpu.SemaphoreType.DMA((PROJ_TM // ROW_CHUNK,))],
        compiler_params=_params("parallel", "arbitrary"),
        name="ret_projection",
    )(x2d, mod, norm_g, w_in, cos, sin)


def _scan_kernel(q_ref, k_ref, v_ref, g_ref, intra_ref, qdec_ref, kdec_ref, cdec_ref, o_ref, state_ref):
    @pl.when(pl.program_id(2) == 0)
    def _():
        state_ref[...] = jnp.zeros_like(state_ref)

    intra = intra_ref[...]
    q_dec = qdec_ref[...]
    k_dec = kdec_ref[...]
    chunk_dec = cdec_ref[...]
    state = state_ref[...]
    for c in range(q_ref.shape[0] // RET_CHUNK):
        rows = slice(c * RET_CHUNK, (c + 1) * RET_CHUNK)
        qc = q_ref[rows, :]
        kc = k_ref[rows, :]
        vc = v_ref[rows, :]
        scores = lax.dot_general(qc, kc, (((1,), (1,)), ((), ())), preferred_element_type=jnp.float32)
        scores = (scores * intra).astype(jnp.bfloat16)
        inner = jnp.dot(scores, vc, preferred_element_type=jnp.float32)
        cross = jnp.dot(qc, state.astype(jnp.bfloat16), preferred_element_type=jnp.float32) * q_dec
        k_decayed = (kc.astype(jnp.float32) * k_dec).astype(jnp.bfloat16)
        state = chunk_dec * state + lax.dot_general(
            k_decayed, vc, (((0,), (0,)), ((), ())), preferred_element_type=jnp.float32)
        y = inner + cross
        mu = jnp.mean(y, axis=-1, keepdims=True)
        yc = y - mu
        var = jnp.mean(yc * yc, axis=-1, keepdims=True)
        yn = yc * lax.rsqrt(var + EPS)
        o_ref[rows, :] = (jax.nn.silu(g_ref[rows, :].astype(jnp.float32)) * yn).astype(o_ref.dtype)
    state_ref[...] = state


def _ret_scan(proj, batch, seq, intra, q_dec, k_dec, chunk_dec):
    n_tok = proj.shape[0]
    steps = seq // RET_TS
    k_off = RET_QK_DIM // RET_HEAD_QK
    v_off = 2 * RET_QK_DIM // RET_HEAD_V
    g_off = v_off + RET_HEADS
    return pl.pallas_call(
        _scan_kernel,
        out_shape=jax.ShapeDtypeStruct((n_tok, RET_V_DIM), jnp.bfloat16),
        grid=(batch, RET_HEADS, steps),
        in_specs=[
            pl.BlockSpec((RET_TS, RET_HEAD_QK), lambda b, h, t: (b * steps + t, h)),
            pl.BlockSpec((RET_TS, RET_HEAD_QK), lambda b, h, t: (b * steps + t, k_off + h)),
            pl.BlockSpec((RET_TS, RET_HEAD_V), lambda b, h, t: (b * steps + t, v_off + h)),
            pl.BlockSpec((RET_TS, RET_HEAD_V), lambda b, h, t: (b * steps + t, g_off + h)),
            pl.BlockSpec((None, RET_CHUNK, RET_CHUNK), lambda b, h, t: (h, 0, 0)),
            pl.BlockSpec((None, RET_CHUNK, 1), lambda b, h, t: (h, 0, 0)),
            pl.BlockSpec((None, RET_CHUNK, 1), lambda b, h, t: (h, 0, 0)),
            pl.BlockSpec((None, 1, 1), lambda b, h, t: (h, 0, 0)),
        ],
        out_specs=pl.BlockSpec((RET_TS, RET_HEAD_V), lambda b, h, t: (b * steps + t, h)),
        scratch_shapes=[pltpu.VMEM((RET_HEAD_QK, RET_HEAD_V), jnp.float32)],
        compiler_params=_params("parallel", "parallel", "arbitrary"),
        name="ret_scan",
    )(proj, proj, proj, proj, intra, q_dec, k_dec, chunk_dec)


def _out_kernel(z_ref, w_ref, x_ref, gate_ref, o_ref, wb_ref):
    @pl.when(pl.program_id(1) == 0)
    def _():
        wb_ref[...] = w_ref[...].astype(jnp.bfloat16)

    y = jnp.dot(z_ref[...], wb_ref[...], preferred_element_type=jnp.float32)
    o_ref[...] = x_ref[...] + gate_ref[0] * y


def _ret_output(z, x2d, mod, mod_idx, seq, w_out, layer):
    n_tok = x2d.shape[0]
    tiles_per_seq = seq // OUT_TM
    gate_off = 2 * D_MODEL // OUT_TN
    return pl.pallas_call(
        _out_kernel,
        out_shape=jax.ShapeDtypeStruct(x2d.shape, jnp.float32),
        grid=(D_MODEL // OUT_TN, n_tok // OUT_TM),
        in_specs=[
            pl.BlockSpec((OUT_TM, RET_V_DIM), lambda j, i: (i, 0)),
            pl.BlockSpec((None, RET_V_DIM, OUT_TN), lambda j, i: (layer, 0, j)),
            pl.BlockSpec((OUT_TM, OUT_TN), lambda j, i: (i, j)),
            pl.BlockSpec((1, 1, OUT_TN), lambda j, i: (mod_idx + i // tiles_per_seq, 0, gate_off + j)),
        ],
        out_specs=pl.BlockSpec((OUT_TM, OUT_TN), lambda j, i: (i, j)),
        scratch_shapes=[pltpu.VMEM((RET_V_DIM, OUT_TN), jnp.bfloat16)],
        compiler_params=_params("parallel", "arbitrary"),
        name="ret_output",
    )(z, w_out, x2d, mod)


def _rotary_tables(seq):
    inv = ROPE_BASE ** (-np.arange(0, RET_HEAD_QK, 2, dtype=np.float64) / RET_HEAD_QK)
    ang = np.arange(seq, dtype=np.float64)[:, None] * inv[None, :]
    cos = np.repeat(np.cos(ang), 2, axis=1)
    sin = np.stack([-np.sin(ang), np.sin(ang)], axis=-1).reshape(seq, RET_HEAD_QK)
    return cos.astype(np.float32), sin.astype(np.float32)


def _decay_tables():
    gamma = 1.0 - 2.0 ** (-5.0 - np.arange(RET_HEADS, dtype=np.float64))
    log_g = np.log(gamma)
    j = np.arange(RET_CHUNK, dtype=np.float64)
    diff = j[:, None] - j[None, :]
    intra = np.where(diff[None] >= 0, np.exp(np.maximum(diff, 0.0)[None] * log_g[:, None, None]), 0.0)
    q_dec = np.exp((j[None, :] + 1.0) * log_g[:, None])[:, :, None]
    k_dec = np.exp((RET_CHUNK - 1.0 - j[None, :]) * log_g[:, None])[:, :, None]
    chunk_dec = np.exp(RET_CHUNK * log_g)[:, None, None]
    return tuple(t.astype(np.float32) for t in (intra, q_dec, k_dec, chunk_dec))


@jax.jit
def kernel(x, c, ada_w, ada_b, norm_g, pool_w, pool_scale, ret_w_in, ret_w_out, mlp_w1, mlp_w2, final_g):
    batch, seq, d = x.shape
    assert d == D_MODEL and batch <= 8
    assert seq % max(POOL_TM, MLP_TM, PROJ_TM, OUT_TM, RET_TS) == 0

    c_pad = jnp.zeros((8, d), jnp.float32).at[:batch].set(c)
    mod = _ada_modulation(c_pad, ada_w.reshape(DEPTH * 2, d, 3 * d), ada_b.reshape(DEPTH * 2, 1, 3 * d))
    mod = mod[:, :batch].reshape(DEPTH * 2 * batch, 1, 3 * d)

    cos, sin = _rotary_tables(seq)
    intra, q_dec, k_dec, chunk_dec = _decay_tables()
    norm_rows = norm_g.reshape(DEPTH * 2, 1, d)
    scale_rows = pool_scale.reshape(-1, 1, d)
    final_row = final_g.reshape(1, d)

    x2d = x.reshape(batch * seq, d)
    for i in range(DEPTH):
        mix_idx = (2 * i) * batch
        mlp_idx = (2 * i + 1) * batch
        if i % 2 == 0:
            x2d = _pool_layer(x2d, mod, mix_idx, seq, norm_rows, 2 * i, pool_w, scale_rows, i // 2)
        else:
            proj = _ret_projection(x2d, mod, mix_idx, seq, norm_rows, 2 * i, ret_w_in, i // 2, cos, sin)
            z = _ret_scan(proj, batch, seq, intra, q_dec, k_dec, chunk_dec)
            x2d = _ret_output(z, x2d, mod, mix_idx, seq, ret_w_out, i // 2)
        x2d = _mlp_layer(x2d, mod, mlp_idx, seq, norm_rows, 2 * i + 1, mlp_w1, mlp_w2, i, final_row,
                         final_norm=(i == DEPTH - 1))
    return x2d.reshape(batch, seq, d)
```

```python
import functools

import jax
import jax.numpy as jnp
import numpy as np
from jax import lax
from jax.experimental import pallas as pl
from jax.experimental.pallas import tpu as pltpu

D_MODEL = 2048
DEPTH = 4
POOL_WINDOWS = (2, 4, 8, 16)
POOL_GROUP = D_MODEL // len(POOL_WINDOWS)
POOL_HALO = 16
RET_HEADS = 8
RET_QK_DIM = D_MODEL
RET_V_DIM = 2 * D_MODEL
RET_HEAD_QK = RET_QK_DIM // RET_HEADS
RET_HEAD_V = RET_V_DIM // RET_HEADS
RET_PROJ = 2 * RET_QK_DIM + 2 * RET_V_DIM
ROPE_BASE = 10000.0
D_FF = 4 * D_MODEL
EPS = 1e-6
LANES = 128

V7X_VMEM_BYTES = 64 * 1024 * 1024
VMEM_LIMIT_BYTES = V7X_VMEM_BYTES - 4 * 1024 * 1024

ROW_CHUNK = 128
ADA_TK = 1024
POOL_TM = 512
MLP_TM = 1024
MLP_TF = 1024
MLP_TN = 512
PROJ_TM = 1024
PROJ_TN = 1024
RET_TS = 1024
RET_CHUNK = 256
OUT_TM = 1024
OUT_TN = 512


def _params(*semantics):
    return pltpu.CompilerParams(dimension_semantics=semantics, vmem_limit_bytes=VMEM_LIMIT_BYTES)


def _norm_modulate(x, g, mod_row):
    shift = mod_row[:, 0:D_MODEL]
    scale = mod_row[:, D_MODEL:2 * D_MODEL]
    inv = lax.rsqrt(jnp.mean(x * x, axis=-1, keepdims=True) + EPS)
    return x * inv * (g * (1.0 + scale)) + shift


def _ada_kernel(c_ref, w_ref, b_ref, o_ref):
    cond = jax.nn.silu(c_ref[...]).astype(jnp.bfloat16)
    w = w_ref[...].astype(jnp.bfloat16)
    o_ref[...] = jnp.dot(cond, w, preferred_element_type=jnp.float32) + b_ref[...]


def _ada_modulation(c_pad, ada_w, ada_b):
    n_ls = ada_w.shape[0]
    n_out = ada_w.shape[2]
    return pl.pallas_call(
        _ada_kernel,
        out_shape=jax.ShapeDtypeStruct((n_ls, 8, n_out), jnp.float32),
        grid=(n_ls, n_out // ADA_TK),
        in_specs=[
            pl.BlockSpec((8, D_MODEL), lambda l, j: (0, 0)),
            pl.BlockSpec((None, D_MODEL, ADA_TK), lambda l, j: (l, 0, j)),
            pl.BlockSpec((None, 1, ADA_TK), lambda l, j: (l, 0, j)),
        ],
        out_specs=pl.BlockSpec((None, 8, ADA_TK), lambda l, j: (l, 0, j)),
        compiler_params=_params("parallel", "parallel"),
        name="ada_modulation",
    )(c_pad, ada_w, ada_b)


def _pool_kernel(tiles_per_seq, x_ref, halo_ref, mod_ref, g_ref, w_ref, ls_ref, o_ref, wb_ref):
    i = pl.program_id(0)

    @pl.when(i == 0)
    def _():
        wb_ref[...] = w_ref[...].astype(jnp.bfloat16)

    tile_in_seq = i % tiles_per_seq
    tm = x_ref.shape[0]
    mod_row = mod_ref[0]
    g = g_ref[0]
    x = x_ref[...]
    h = _norm_modulate(x, g, mod_row)
    h_halo = _norm_modulate(halo_ref[...], g, mod_row)
    h_halo = jnp.where(tile_in_seq == 0, 0.0, h_halo)
    out_scale = mod_row[:, 2 * D_MODEL:3 * D_MODEL] * ls_ref[0]
    pos = tile_in_seq * tm + lax.broadcasted_iota(jnp.int32, (tm, 1), 0)
    for gi, window in enumerate(POOL_WINDOWS):
        cols = slice(gi * POOL_GROUP, (gi + 1) * POOL_GROUP)
        hg = h[:, cols]
        acc = jnp.concatenate([h_halo[:, cols], hg], axis=0)
        shift = 1
        while shift < window:
            acc = acc + pltpu.roll(acc, shift, axis=0)
            shift *= 2
        cnt = jnp.minimum(pos + 1, window).astype(jnp.float32)
        p = acc[POOL_HALO:, :] / cnt - hg
        y = jnp.dot(p.astype(jnp.bfloat16), wb_ref[gi], preferred_element_type=jnp.float32)
        o_ref[:, cols] = x[:, cols] + out_scale[:, cols] * y


def _pool_layer(x2d, mod, mod_idx, seq, norm_g, norm_idx, pool_w, pool_scale, pool_idx):
    n_tok = x2d.shape[0]
    tiles_per_seq = seq // POOL_TM
    halo_blocks_per_tile = POOL_TM // POOL_HALO
    n_grp = len(POOL_WINDOWS)
    return pl.pallas_call(
        functools.partial(_pool_kernel, tiles_per_seq),
        out_shape=jax.ShapeDtypeStruct(x2d.shape, jnp.float32),
        grid=(n_tok // POOL_TM,),
        in_specs=[
            pl.BlockSpec((POOL_TM, D_MODEL), lambda i: (i, 0)),
            pl.BlockSpec((POOL_HALO, D_MODEL), lambda i: (jnp.maximum(i * halo_blocks_per_tile - 1, 0), 0)),
            pl.BlockSpec((1, 1, 3 * D_MODEL), lambda i: (mod_idx + i // tiles_per_seq, 0, 0)),
            pl.BlockSpec((1, 1, D_MODEL), lambda i: (norm_idx, 0, 0)),
            pl.BlockSpec((None, n_grp, POOL_GROUP, POOL_GROUP), lambda i: (pool_idx, 0, 0, 0)),
            pl.BlockSpec((1, 1, D_MODEL), lambda i: (pool_idx, 0, 0)),
        ],
        out_specs=pl.BlockSpec((POOL_TM, D_MODEL), lambda i: (i, 0)),
        scratch_shapes=[pltpu.VMEM((n_grp, POOL_GROUP, POOL_GROUP), jnp.bfloat16)],
        compiler_params=_params("arbitrary"),
        name="pool_mixer",
    )(x2d, x2d, mod, norm_g, pool_w, pool_scale)


def _row_chunk_copies(x_hbm, first_row, dst_of_chunk, sems):
    return [
        pltpu.make_async_copy(x_hbm.at[pl.ds(first_row + k * ROW_CHUNK, ROW_CHUNK), :], dst_of_chunk(k), sems.at[k])
        for k in range(sems.shape[0])
    ]


def _mlp_kernel(final_norm, x_hbm, mod_ref, g_ref, w1_ref, w2_ref, fg_ref, o_ref, h_ref, sems):
    j = pl.program_id(1)

    @pl.when(j == 0)
    def _():
        first_row = pl.multiple_of(pl.program_id(0) * MLP_TM, MLP_TM)
        copies = _row_chunk_copies(x_hbm, first_row, lambda k: o_ref.at[pl.ds(k * ROW_CHUNK, ROW_CHUNK), :], sems)
        for cp in copies:
            cp.start()
        for k, cp in enumerate(copies):
            cp.wait()
            rows = pl.ds(k * ROW_CHUNK, ROW_CHUNK)
            h_ref[rows, :] = _norm_modulate(o_ref[rows, :], g_ref[0], mod_ref[0]).astype(jnp.bfloat16)

    a = jnp.dot(h_ref[...], w1_ref[...].astype(jnp.bfloat16), preferred_element_type=jnp.float32)
    a = jnp.maximum(a, 0.0)
    a = (a * a).astype(jnp.bfloat16)
    w2 = w2_ref[...].astype(jnp.bfloat16)
    for n in range(0, D_MODEL, MLP_TN):
        gate = mod_ref[0, :, 2 * D_MODEL + n:2 * D_MODEL + n + MLP_TN]
        o_ref[:, n:n + MLP_TN] += gate * jnp.dot(a, w2[:, n:n + MLP_TN], preferred_element_type=jnp.float32)

    if final_norm:
        @pl.when(j == pl.num_programs(1) - 1)
        def _():
            y = o_ref[...]
            inv = lax.rsqrt(jnp.mean(y * y, axis=-1, keepdims=True) + EPS)
            o_ref[...] = y * inv * fg_ref[...]


def _mlp_layer(x2d, mod, mod_idx, seq, norm_g, norm_idx, w1, w2, layer, final_g, final_norm):
    n_tok = x2d.shape[0]
    tiles_per_seq = seq // MLP_TM
    return pl.pallas_call(
        functools.partial(_mlp_kernel, final_norm),
        out_shape=jax.ShapeDtypeStruct(x2d.shape, jnp.float32),
        grid=(n_tok // MLP_TM, D_FF // MLP_TF),
        in_specs=[
            pl.BlockSpec(memory_space=pl.ANY),
            pl.BlockSpec((1, 1, 3 * D_MODEL), lambda i, j: (mod_idx + i // tiles_per_seq, 0, 0)),
            pl.BlockSpec((1, 1, D_MODEL), lambda i, j: (norm_idx, 0, 0)),
            pl.BlockSpec((None, D_MODEL, MLP_TF), lambda i, j: (layer, 0, j)),
            pl.BlockSpec((None, MLP_TF, D_MODEL), lambda i, j: (layer, j, 0)),
            pl.BlockSpec((1, D_MODEL), lambda i, j: (0, 0)),
        ],
        out_specs=pl.BlockSpec((MLP_TM, D_MODEL), lambda i, j: (i, 0)),
        scratch_shapes=[pltpu.VMEM((MLP_TM, D_MODEL), jnp.bfloat16),
                        pltpu.SemaphoreType.DMA((MLP_TM // ROW_CHUNK,))],
        compiler_params=_params("parallel", "arbitrary"),
        name="mlp_final" if final_norm else "mlp",
    )(x2d, mod, norm_g, w1, w2, final_g)


def _proj_kernel(x_hbm, mod_ref, g_ref, w_ref, cos_ref, sin_ref, o_ref, h_ref, xbuf_ref, sems):
    j = pl.program_id(1)
    n_q = RET_QK_DIM // PROJ_TN

    @pl.when(j == 0)
    def _():
        first_row = pl.multiple_of(pl.program_id(0) * PROJ_TM, PROJ_TM)
        copies = _row_chunk_copies(x_hbm, first_row, lambda k: xbuf_ref.at[k], sems)
        for cp in copies:
            cp.start()
        for k, cp in enumerate(copies):
            cp.wait()
            rows = pl.ds(k * ROW_CHUNK, ROW_CHUNK)
            h_ref[rows, :] = _norm_modulate(xbuf_ref[k], g_ref[0], mod_ref[0]).astype(jnp.bfloat16)

    @pl.when(j < 2 * n_q)
    def _():
        k_scale = jnp.where(j >= n_q, RET_HEAD_QK ** -0.5, 1.0)
        cos = cos_ref[...] * k_scale
        sin = sin_ref[...] * k_scale
        swap = lax.broadcasted_iota(jnp.int32, (PROJ_TM, LANES), 1) ^ 1
        r = jnp.dot(h_ref[...], w_ref[...].astype(jnp.bfloat16), preferred_element_type=jnp.float32)
        for lo in range(0, PROJ_TN, LANES):
            t = lo % RET_HEAD_QK
            xs = r[:, lo:lo + LANES]
            partner = jnp.take_along_axis(xs, swap, axis=1)
            o_ref[:, lo:lo + LANES] = (xs * cos[:, t:t + LANES] + partner * sin[:, t:t + LANES]).astype(o_ref.dtype)

    @pl.when(j >= 2 * n_q)
    def _():
        w = w_ref[...].astype(jnp.bfloat16)
        o_ref[...] = jnp.dot(h_ref[...], w, preferred_element_type=jnp.float32).astype(o_ref.dtype)


def _ret_projection(x2d, mod, mod_idx, seq, norm_g, norm_idx, w_in, layer, cos, sin):
    n_tok = x2d.shape[0]
    tiles_per_seq = seq // PROJ_TM
    return pl.pallas_call(
        _proj_kernel,
        out_shape=jax.ShapeDtypeStruct((n_tok, RET_PROJ), jnp.bfloat16),
        grid=(n_tok // PROJ_TM, RET_PROJ // PROJ_TN),
        in_specs=[
            pl.BlockSpec(memory_space=pl.ANY),
            pl.BlockSpec((1, 1, 3 * D_MODEL), lambda i, j: (mod_idx + i // tiles_per_seq, 0, 0)),
            pl.BlockSpec((1, 1, D_MODEL), lambda i, j: (norm_idx, 0, 0)),
            pl.BlockSpec((None, D_MODEL, PROJ_TN), lambda i, j: (layer, 0, j)),
            pl.BlockSpec((PROJ_TM, RET_HEAD_QK), lambda i, j: (i % tiles_per_seq, 0)),
            pl.BlockSpec((PROJ_TM, RET_HEAD_QK), lambda i, j: (i % tiles_per_seq, 0)),
        ],
        out_specs=pl.BlockSpec((PROJ_TM, PROJ_TN), lambda i, j: (i, j)),
        scratch_shapes=[pltpu.VMEM((PROJ_TM, D_MODEL), jnp.bfloat16),
                        pltpu.VMEM((PROJ_TM // ROW_CHUNK, ROW_CHUNK, D_MODEL), jnp.float32),
                        pltpu.SemaphoreType.DMA((PROJ_TM // ROW_CHUNK,))],
        compiler_params=_params("parallel", "arbitrary"),
        name="ret_projection",
    )(x2d, mod, norm_g, w_in, cos, sin)


def _scan_kernel(q_ref, k_ref, v_ref, g_ref, intra_ref, qdec_ref, kdec_ref, cdec_ref, o_ref, state_ref):
    @pl.when(pl.program_id(2) == 0)
    def _():
        state_ref[...] = jnp.zeros_like(state_ref)

    intra = intra_ref[...]
    q_dec = qdec_ref[...]
    k_dec = kdec_ref[...]
    chunk_dec = cdec_ref[...]
    state = state_ref[...]
    for c in range(q_ref.shape[0] // RET_CHUNK):
        rows = slice(c * RET_CHUNK, (c + 1) * RET_CHUNK)
        qc = q_ref[rows, :]
        kc = k_ref[rows, :]
        vc = v_ref[rows, :]
        scores = lax.dot_general(qc, kc, (((1,), (1,)), ((), ())), preferred_element_type=jnp.float32)
        scores = (scores * intra).astype(jnp.bfloat16)
        inner = jnp.dot(scores, vc, preferred_element_type=jnp.float32)
        cross = jnp.dot(qc, state.astype(jnp.bfloat16), preferred_element_type=jnp.float32) * q_dec
        k_decayed = (kc.astype(jnp.float32) * k_dec).astype(jnp.bfloat16)
        state = chunk_dec * state + lax.dot_general(
            k_decayed, vc, (((0,), (0,)), ((), ())), preferred_element_type=jnp.float32)
        y = inner + cross
        mu = jnp.mean(y, axis=-1, keepdims=True)
        yc = y - mu
        var = jnp.mean(yc * yc, axis=-1, keepdims=True)
        yn = yc * lax.rsqrt(var + EPS)
        o_ref[rows, :] = (jax.nn.silu(g_ref[rows, :].astype(jnp.float32)) * yn).astype(o_ref.dtype)
    state_ref[...] = state


def _ret_scan(proj, batch, seq, intra, q_dec, k_dec, chunk_dec):
    n_tok = proj.shape[0]
    steps = seq // RET_TS
    k_off = RET_QK_DIM // RET_HEAD_QK
    v_off = 2 * RET_QK_DIM // RET_HEAD_V
    g_off = v_off + RET_HEADS
    return pl.pallas_call(
        _scan_kernel,
        out_shape=jax.ShapeDtypeStruct((n_tok, RET_V_DIM), jnp.bfloat16),
        grid=(batch, RET_HEADS, steps),
        in_specs=[
            pl.BlockSpec((RET_TS, RET_HEAD_QK), lambda b, h, t: (b * steps + t, h)),
            pl.BlockSpec((RET_TS, RET_HEAD_QK), lambda b, h, t: (b * steps + t, k_off + h)),
            pl.BlockSpec((RET_TS, RET_HEAD_V), lambda b, h, t: (b * steps + t, v_off + h)),
            pl.BlockSpec((RET_TS, RET_HEAD_V), lambda b, h, t: (b * steps + t, g_off + h)),
            pl.BlockSpec((None, RET_CHUNK, RET_CHUNK), lambda b, h, t: (h, 0, 0)),
            pl.BlockSpec((None, RET_CHUNK, 1), lambda b, h, t: (h, 0, 0)),
            pl.BlockSpec((None, RET_CHUNK, 1), lambda b, h, t: (h, 0, 0)),
            pl.BlockSpec((None, 1, 1), lambda b, h, t: (h, 0, 0)),
        ],
        out_specs=pl.BlockSpec((RET_TS, RET_HEAD_V), lambda b, h, t: (b * steps + t, h)),
        scratch_shapes=[pltpu.VMEM((RET_HEAD_QK, RET_HEAD_V), jnp.float32)],
        compiler_params=_params("parallel", "parallel", "arbitrary"),
        name="ret_scan",
    )(proj, proj, proj, proj, intra, q_dec, k_dec, chunk_dec)


def _out_kernel(z_ref, w_ref, x_ref, gate_ref, o_ref, wb_ref):
    @pl.when(pl.program_id(1) == 0)
    def _():
        wb_ref[...] = w_ref[...].astype(jnp.bfloat16)

    y = jnp.dot(z_ref[...], wb_ref[...], preferred_element_type=jnp.float32)
    o_ref[...] = x_ref[...] + gate_ref[0] * y


def _ret_output(z, x2d, mod, mod_idx, seq, w_out, layer):
    n_tok = x2d.shape[0]
    tiles_per_seq = seq // OUT_TM
    gate_off = 2 * D_MODEL // OUT_TN
    return pl.pallas_call(
        _out_kernel,
        out_shape=jax.ShapeDtypeStruct(x2d.shape, jnp.float32),
        grid=(D_MODEL // OUT_TN, n_tok // OUT_TM),
        in_specs=[
            pl.BlockSpec((OUT_TM, RET_V_DIM), lambda j, i: (i, 0)),
            pl.BlockSpec((None, RET_V_DIM, OUT_TN), lambda j, i: (layer, 0, j)),
            pl.BlockSpec((OUT_TM, OUT_TN), lambda j, i: (i, j)),
            pl.BlockSpec((1, 1, OUT_TN), lambda j, i: (mod_idx + i // tiles_per_seq, 0, gate_off + j)),
        ],
        out_specs=pl.BlockSpec((OUT_TM, OUT_TN), lambda j, i: (i, j)),
        scratch_shapes=[pltpu.VMEM((RET_V_DIM, OUT_TN), jnp.bfloat16)],
        compiler_params=_params("parallel", "arbitrary"),
        name="ret_output",
    )(z, w_out, x2d, mod)


def _rotary_tables(seq):
    inv = ROPE_BASE ** (-np.arange(0, RET_HEAD_QK, 2, dtype=np.float64) / RET_HEAD_QK)
    ang = np.arange(seq, dtype=np.float64)[:, None] * inv[None, :]
    cos = np.repeat(np.cos(ang), 2, axis=1)
    sin = np.stack([-np.sin(ang), np.sin(ang)], axis=-1).reshape(seq, RET_HEAD_QK)
    return cos.astype(np.float32), sin.astype(np.float32)


def _decay_tables():
    gamma = 1.0 - 2.0 ** (-5.0 - np.arange(RET_HEADS, dtype=np.float64))
    log_g = np.log(gamma)
    j = np.arange(RET_CHUNK, dtype=np.float64)
    diff = j[:, None] - j[None, :]
    intra = np.where(diff[None] >= 0, np.exp(np.maximum(diff, 0.0)[None] * log_g[:, None, None]), 0.0)
    q_dec = np.exp((j[None, :] + 1.0) * log_g[:, None])[:, :, None]
    k_dec = np.exp((RET_CHUNK - 1.0 - j[None, :]) * log_g[:, None])[:, :, None]
    chunk_dec = np.exp(RET_CHUNK * log_g)[:, None, None]
    return tuple(t.astype(np.float32) for t in (intra, q_dec, k_dec, chunk_dec))


@jax.jit
def kernel(x, c, ada_w, ada_b, norm_g, pool_w, pool_scale, ret_w_in, ret_w_out, mlp_w1, mlp_w2, final_g):
    batch, seq, d = x.shape
    assert d == D_MODEL and batch <= 8
    assert seq % max(POOL_TM, MLP_TM, PROJ_TM, OUT_TM, RET_TS) == 0

    c_pad = jnp.zeros((8, d), jnp.float32).at[:batch].set(c)
    mod = _ada_modulation(c_pad, ada_w.reshape(DEPTH * 2, d, 3 * d), ada_b.reshape(DEPTH * 2, 1, 3 * d))
    mod = mod[:, :batch].reshape(DEPTH * 2 * batch, 1, 3 * d)

    cos, sin = _rotary_tables(seq)
    intra, q_dec, k_dec, chunk_dec = _decay_tables()
    norm_rows = norm_g.reshape(DEPTH * 2, 1, d)
    scale_rows = pool_scale.reshape(-1, 1, d)
    final_row = final_g.reshape(1, d)

    x2d = x.reshape(batch * seq, d)
    for i in range(DEPTH):
        mix_idx = (2 * i) * batch
        mlp_idx = (2 * i + 1) * batch
        if i % 2 == 0:
            x2d = _pool_layer(x2d, mod, mix_idx, seq, norm_rows, 2 * i, pool_w, scale_rows, i // 2)
        else:
            proj = _ret_projection(x2d, mod, mix_idx, seq, norm_rows, 2 * i, ret_w_in, i // 2, cos, sin)
            z = _ret_scan(proj, batch, seq, intra, q_dec, k_dec, chunk_dec)
            x2d = _ret_output(z, x2d, mod, mix_idx, seq, ret_w_out, i // 2)
        x2d = _mlp_layer(x2d, mod, mlp_idx, seq, norm_rows, 2 * i + 1, mlp_w1, mlp_w2, i, final_row,
                         final_norm=(i == DEPTH - 1))
    return x2d.reshape(batch, seq, d)
```

```python
import functools

import jax
import jax.numpy as jnp
import numpy as np
from jax import lax
from jax.experimental import pallas as pl
from jax.experimental.pallas import tpu as pltpu

D_MODEL = 2048
DEPTH = 4
POOL_WINDOWS = (2, 4, 8, 16)
POOL_GROUP = D_MODEL // len(POOL_WINDOWS)
POOL_HALO = 16
RET_HEADS = 8
RET_QK_DIM = D_MODEL
RET_V_DIM = 2 * D_MODEL
RET_HEAD_QK = RET_QK_DIM // RET_HEADS
RET_HEAD_V = RET_V_DIM // RET_HEADS
RET_PROJ = 2 * RET_QK_DIM + 2 * RET_V_DIM
ROPE_BASE = 10000.0
D_FF = 4 * D_MODEL
EPS = 1e-6
LANES = 128

V7X_VMEM_BYTES = 64 * 1024 * 1024
VMEM_LIMIT_BYTES = V7X_VMEM_BYTES - 4 * 1024 * 1024

ROW_CHUNK = 128
ADA_TK = 1024
ADA_FUSED_TK = 128
POOL_TM = 512
MLP_TM = 1024
MLP_TF = 512
MLP_TN = 512
PROJ_TM = 1024
PROJ_TN = 1024
RET_TS = 1024
RET_CHUNK = 256
OUT_TM = 1024
OUT_TN = 512


def _params(*semantics):
    return pltpu.CompilerParams(dimension_semantics=semantics, vmem_limit_bytes=VMEM_LIMIT_BYTES)


def _norm_modulate(x, g, mod_row):
    shift = mod_row[:, 0:D_MODEL]
    scale = mod_row[:, D_MODEL:2 * D_MODEL]
    inv = lax.rsqrt(jnp.mean(x * x, axis=-1, keepdims=True) + EPS)
    return x * inv * (g * (1.0 + scale)) + shift


def _ada_columns(w, cond_ref, bias):
    rows = []
    for b in range(cond_ref.shape[0]):
        cond = cond_ref[b]
        rows.append(jnp.concatenate(
            [jnp.sum(w[:, lo:lo + LANES] * cond, axis=0, keepdims=True) for lo in range(0, w.shape[1], LANES)],
            axis=1))
    return jnp.concatenate(rows, axis=0) + bias


def _ada_kernel(c_ref, w_ref, b_ref, o_ref, cond_ref):
    cond_ref[...] = jax.nn.silu(c_ref[...])
    o_ref[...] = _ada_columns(w_ref[...], cond_ref, b_ref[...])


def _ada_first_layer(c_lanes, ada_w, ada_b):
    batch = c_lanes.shape[0]
    n_out = ada_w.shape[2]
    return pl.pallas_call(
        _ada_kernel,
        out_shape=(jax.ShapeDtypeStruct((2, batch, n_out), jnp.float32),
                   jax.ShapeDtypeStruct(c_lanes.shape, jnp.float32)),
        grid=(2, n_out // ADA_TK),
        in_specs=[
            pl.BlockSpec(c_lanes.shape, lambda l, j: (0, 0, 0)),
            pl.BlockSpec((None, D_MODEL, ADA_TK), lambda l, j: (l, 0, j)),
            pl.BlockSpec((None, 1, ADA_TK), lambda l, j: (l, 0, j)),
        ],
        out_specs=(pl.BlockSpec((None, batch, ADA_TK), lambda l, j: (l, 0, j)),
                   pl.BlockSpec(c_lanes.shape, lambda l, j: (0, 0, 0))),
        compiler_params=_params("arbitrary", "arbitrary"),
        name="ada_modulation",
    )(c_lanes, ada_w, ada_b)


def _pool_kernel(tiles_per_seq, x_ref, halo_ref, mod_ref, g_ref, w_ref, ls_ref, o_ref, wb_ref):
    i = pl.program_id(0)

    @pl.when(i == 0)
    def _():
        wb_ref[...] = w_ref[...].astype(jnp.bfloat16)

    tile_in_seq = i % tiles_per_seq
    tm = x_ref.shape[0]
    mod_row = mod_ref[0]
    g = g_ref[0]
    x = x_ref[...]
    h = _norm_modulate(x, g, mod_row)
    h_halo = _norm_modulate(halo_ref[...], g, mod_row)
    h_halo = jnp.where(tile_in_seq == 0, 0.0, h_halo)
    out_scale = mod_row[:, 2 * D_MODEL:3 * D_MODEL] * ls_ref[0]
    pos = tile_in_seq * tm + lax.broadcasted_iota(jnp.int32, (tm, 1), 0)
    for gi, window in enumerate(POOL_WINDOWS):
        cols = slice(gi * POOL_GROUP, (gi + 1) * POOL_GROUP)
        hg = h[:, cols]
        acc = jnp.concatenate([h_halo[:, cols], hg], axis=0)
        shift = 1
        while shift < window:
            acc = acc + pltpu.roll(acc, shift, axis=0)
            shift *= 2
        cnt = jnp.minimum(pos + 1, window).astype(jnp.float32)
        p = acc[POOL_HALO:, :] / cnt - hg
        y = jnp.dot(p.astype(jnp.bfloat16), wb_ref[gi], preferred_element_type=jnp.float32)
        o_ref[:, cols] = x[:, cols] + out_scale[:, cols] * y


def _pool_layer(x2d, mod, mod_idx, seq, norm_g, norm_idx, pool_w, pool_scale, pool_idx):
    n_tok = x2d.shape[0]
    tiles_per_seq = seq // POOL_TM
    halo_blocks_per_tile = POOL_TM // POOL_HALO
    n_grp = len(POOL_WINDOWS)
    return pl.pallas_call(
        functools.partial(_pool_kernel, tiles_per_seq),
        out_shape=jax.ShapeDtypeStruct(x2d.shape, jnp.float32),
        grid=(n_tok // POOL_TM,),
        in_specs=[
            pl.BlockSpec((POOL_TM, D_MODEL), lambda i: (i, 0)),
            pl.BlockSpec((POOL_HALO, D_MODEL), lambda i: (jnp.maximum(i * halo_blocks_per_tile - 1, 0), 0)),
            pl.BlockSpec((1, 1, 3 * D_MODEL), lambda i: (mod_idx + i // tiles_per_seq, 0, 0)),
            pl.BlockSpec((1, 1, D_MODEL), lambda i: (norm_idx, 0, 0)),
            pl.BlockSpec((None, n_grp, POOL_GROUP, POOL_GROUP), lambda i: (pool_idx, 0, 0, 0)),
            pl.BlockSpec((1, 1, D_MODEL), lambda i: (pool_idx, 0, 0)),
        ],
        out_specs=pl.BlockSpec((POOL_TM, D_MODEL), lambda i: (i, 0)),
        scratch_shapes=[pltpu.VMEM((n_grp, POOL_GROUP, POOL_GROUP), jnp.bfloat16)],
        compiler_params=_params("arbitrary"),
        name="pool_mixer",
    )(x2d, x2d, mod, norm_g, pool_w, pool_scale)


def _row_chunk_copies(x_hbm, first_row, dst_of_chunk, sems):
    return [
        pltpu.make_async_copy(x_hbm.at[pl.ds(first_row + k * ROW_CHUNK, ROW_CHUNK), :], dst_of_chunk(k), sems.at[k])
        for k in range(sems.shape[0])
    ]


def _mlp_kernel(last, x_hbm, mod_ref, g_ref, w1_ref, w2_ref, *refs):
    if last:
        fg_ref, o_ref, h_ref, sems = refs
    else:
        wa_ref, ba_ref, cond_ref, o_ref, next_mod_ref, h_ref, sems = refs
    j = pl.program_id(1)

    @pl.when(j == 0)
    def _():
        first_row = pl.multiple_of(pl.program_id(0) * MLP_TM, MLP_TM)
        copies = _row_chunk_copies(x_hbm, first_row, lambda k: o_ref.at[pl.ds(k * ROW_CHUNK, ROW_CHUNK), :], sems)
        for cp in copies:
            cp.start()
        for k, cp in enumerate(copies):
            cp.wait()
            rows = pl.ds(k * ROW_CHUNK, ROW_CHUNK)
            h_ref[rows, :] = _norm_modulate(o_ref[rows, :], g_ref[0], mod_ref[0]).astype(jnp.bfloat16)

    if not last:
        next_mod_ref[...] = _ada_columns(wa_ref[...], cond_ref, ba_ref[...])

    a = jnp.dot(h_ref[...], w1_ref[...].astype(jnp.bfloat16), preferred_element_type=jnp.float32)
    a = jnp.maximum(a, 0.0)
    a = (a * a).astype(jnp.bfloat16)
    w2 = w2_ref[...].astype(jnp.bfloat16)
    for n in range(0, D_MODEL, MLP_TN):
        gate = mod_ref[0, :, 2 * D_MODEL + n:2 * D_MODEL + n + MLP_TN]
        o_ref[:, n:n + MLP_TN] += gate * jnp.dot(a, w2[:, n:n + MLP_TN], preferred_element_type=jnp.float32)

    if last:
        @pl.when(j == pl.num_programs(1) - 1)
        def _():
            y = o_ref[...]
            inv = lax.rsqrt(jnp.mean(y * y, axis=-1, keepdims=True) + EPS)
            o_ref[...] = y * inv * fg_ref[...]


def _mlp_layer(x2d, mod, mod_idx, seq, norm_g, norm_idx, w1, w2, layer, final_g=None, ada=None):
    n_tok = x2d.shape[0]
    tiles_per_seq = seq // MLP_TM
    steps = D_FF // MLP_TF
    last = ada is None
    in_specs = [
        pl.BlockSpec(memory_space=pl.ANY),
        pl.BlockSpec((1, 1, 3 * D_MODEL), lambda i, j: (mod_idx + i // tiles_per_seq, 0, 0)),
        pl.BlockSpec((1, 1, D_MODEL), lambda i, j: (norm_idx, 0, 0)),
        pl.BlockSpec((None, D_MODEL, MLP_TF), lambda i, j: (layer, 0, j)),
        pl.BlockSpec((None, MLP_TF, D_MODEL), lambda i, j: (layer, j, 0)),
    ]
    x_spec = pl.BlockSpec((MLP_TM, D_MODEL), lambda i, j: (i, 0))
    x_shape = jax.ShapeDtypeStruct(x2d.shape, jnp.float32)
    if last:
        operands = (x2d, mod, norm_g, w1, w2, final_g)
        in_specs.append(pl.BlockSpec((1, D_MODEL), lambda i, j: (0, 0)))
        out_specs, out_shape = x_spec, x_shape
    else:
        ada_w, ada_b, cond = operands_ada = ada
        batch = cond.shape[0]
        blocks_per_row = 3 * D_MODEL // ADA_FUSED_TK
        n_blocks = 2 * blocks_per_row
        assert n_blocks <= (n_tok // MLP_TM) * steps

        def ada_block(i, j):
            blk = jnp.minimum(i * steps + j, n_blocks - 1)
            return 2 * (layer + 1) + blk // blocks_per_row, 0, blk % blocks_per_row

        def next_mod_block(i, j):
            row, _, col = ada_block(i, j)
            return row - 2 * (layer + 1), 0, col

        operands = (x2d, mod, norm_g, w1, w2) + operands_ada
        in_specs += [
            pl.BlockSpec((None, D_MODEL, ADA_FUSED_TK), ada_block),
            pl.BlockSpec((None, 1, ADA_FUSED_TK), ada_block),
            pl.BlockSpec(cond.shape, lambda i, j: (0, 0, 0)),
        ]
        out_specs = (x_spec, pl.BlockSpec((None, batch, ADA_FUSED_TK), next_mod_block))
        out_shape = (x_shape, jax.ShapeDtypeStruct((2, batch, 3 * D_MODEL), jnp.float32))
    return pl.pallas_call(
        functools.partial(_mlp_kernel, last),
        out_shape=out_shape,
        grid=(n_tok // MLP_TM, steps),
        in_specs=in_specs,
        out_specs=out_specs,
        scratch_shapes=[pltpu.VMEM((MLP_TM, D_MODEL), jnp.bfloat16),
                        pltpu.SemaphoreType.DMA((MLP_TM // ROW_CHUNK,))],
        compiler_params=_params("arbitrary", "arbitrary"),
        name="mlp_final" if last else "mlp",
    )(*operands)


def _proj_kernel(x_hbm, mod_ref, g_ref, w_ref, cos_ref, sin_ref, o_ref, h_ref, xbuf_ref, sems):
    j = pl.program_id(1)
    n_q = RET_QK_DIM // PROJ_TN

    @pl.when(j == 0)
    def _():
        first_row = pl.multiple_of(pl.program_id(0) * PROJ_TM, PROJ_TM)
        copies = _row_chunk_copies(x_hbm, first_row, lambda k: xbuf_ref.at[k], sems)
        for cp in copies:
            cp.start()
        for k, cp in enumerate(copies):
            cp.wait()
            rows = pl.ds(k * ROW_CHUNK, ROW_CHUNK)
            h_ref[rows, :] = _norm_modulate(xbuf_ref[k], g_ref[0], mod_ref[0]).astype(jnp.bfloat16)

    @pl.when(j < 2 * n_q)
    def _():
        k_scale = jnp.where(j >= n_q, RET_HEAD_QK ** -0.5, 1.0)
        cos = cos_ref[...] * k_scale
        sin = sin_ref[...] * k_scale
        swap = lax.broadcasted_iota(jnp.int32, (PROJ_TM, LANES), 1) ^ 1
        r = jnp.dot(h_ref[...], w_ref[...].astype(jnp.bfloat16), preferred_element_type=jnp.float32)
        for lo in range(0, PROJ_TN, LANES):
            t = lo % RET_HEAD_QK
            xs = r[:, lo:lo + LANES]
            partner = jnp.take_along_axis(xs, swap, axis=1)
            o_ref[:, lo:lo + LANES] = (xs * cos[:, t:t + LANES] + partner * sin[:, t:t + LANES]).astype(o_ref.dtype)

    @pl.when(j >= 2 * n_q)
    def _():
        w = w_ref[...].astype(jnp.bfloat16)
        o_ref[...] = jnp.dot(h_ref[...], w, preferred_element_type=jnp.float32).astype(o_ref.dtype)


def _ret_projection(x2d, mod, mod_idx, seq, norm_g, norm_idx, w_in, layer, cos, sin):
    n_tok = x2d.shape[0]
    tiles_per_seq = seq // PROJ_TM
    return pl.pallas_call(
        _proj_kernel,
        out_shape=jax.ShapeDtypeStruct((n_tok, RET_PROJ), jnp.bfloat16),
        grid=(n_tok // PROJ_TM, RET_PROJ // PROJ_TN),
        in_specs=[
            pl.BlockSpec(memory_space=pl.ANY),
            pl.BlockSpec((1, 1, 3 * D_MODEL), lambda i, j: (mod_idx + i // tiles_per_seq, 0, 0)),
            pl.BlockSpec((1, 1, D_MODEL), lambda i, j: (norm_idx, 0, 0)),
            pl.BlockSpec((None, D_MODEL, PROJ_TN), lambda i, j: (layer, 0, j)),
            pl.BlockSpec((PROJ_TM, RET_HEAD_QK), lambda i, j: (i % tiles_per_seq, 0)),
            pl.BlockSpec((PROJ_TM, RET_HEAD_QK), lambda i, j: (i % tiles_per_seq, 0)),
        ],
        out_specs=pl.BlockSpec((PROJ_TM, PROJ_TN), lambda i, j: (i, j)),
        scratch_shapes=[pltpu.VMEM((PROJ_TM, D_MODEL), jnp.bfloat16),
                        pltpu.VMEM((PROJ_TM // ROW_CHUNK, ROW_CHUNK, D_MODEL), jnp.float32),
                        pltpu.SemaphoreType.DMA((PROJ_TM // ROW_CHUNK,))],
        compiler_params=_params("parallel", "arbitrary"),
        name="ret_projection",
    )(x2d, mod, norm_g, w_in, cos, sin)


def _scan_kernel(q_ref, k_ref, v_ref, g_ref, intra_ref, qdec_ref, kdec_ref, cdec_ref, o_ref, state_ref):
    @pl.when(pl.program_id(2) == 0)
    def _():
        state_ref[...] = jnp.zeros_like(state_ref)

    intra = intra_ref[...]
    q_dec = qdec_ref[...]
    k_dec = kdec_ref[...]
    chunk_dec = cdec_ref[...]
    state = state_ref[...]
    for c in range(q_ref.shape[0] // RET_CHUNK):
        rows = slice(c * RET_CHUNK, (c + 1) * RET_CHUNK)
        qc = q_ref[rows, :]
        kc = k_ref[rows, :]
        vc = v_ref[rows, :]
        scores = lax.dot_general(qc, kc, (((1,), (1,)), ((), ())), preferred_element_type=jnp.float32)
        scores = (scores * intra).astype(jnp.bfloat16)
        inner = jnp.dot(scores, vc, preferred_element_type=jnp.float32)
        cross = jnp.dot(qc, state.astype(jnp.bfloat16), preferred_element_type=jnp.float32) * q_dec
        k_decayed = (kc.astype(jnp.float32) * k_dec).astype(jnp.bfloat16)
        state = chunk_dec * state + lax.dot_general(
            k_decayed, vc, (((0,), (0,)), ((), ())), preferred_element_type=jnp.float32)
        y = inner + cross
        mu = jnp.mean(y, axis=-1, keepdims=True)
        yc = y - mu
        var = jnp.mean(yc * yc, axis=-1, keepdims=True)
        yn = yc * lax.rsqrt(var + EPS)
        o_ref[rows, :] = (jax.nn.silu(g_ref[rows, :].astype(jnp.float32)) * yn).astype(o_ref.dtype)
    state_ref[...] = state


def _ret_scan(proj, batch, seq, intra, q_dec, k_dec, chunk_dec):
    n_tok = proj.shape[0]
    steps = seq // RET_TS
    k_off = RET_QK_DIM // RET_HEAD_QK
    v_off = 2 * RET_QK_DIM // RET_HEAD_V
    g_off = v_off + RET_HEADS
    return pl.pallas_call(
        _scan_kernel,
        out_shape=jax.ShapeDtypeStruct((n_tok, RET_V_DIM), jnp.bfloat16),
        grid=(batch, RET_HEADS, steps),
        in_specs=[
            pl.BlockSpec((RET_TS, RET_HEAD_QK), lambda b, h, t: (b * steps + t, h)),
            pl.BlockSpec((RET_TS, RET_HEAD_QK), lambda b, h, t: (b * steps + t, k_off + h)),
            pl.BlockSpec((RET_TS, RET_HEAD_V), lambda b, h, t: (b * steps + t, v_off + h)),
            pl.BlockSpec((RET_TS, RET_HEAD_V), lambda b, h, t: (b * steps + t, g_off + h)),
            pl.BlockSpec((None, RET_CHUNK, RET_CHUNK), lambda b, h, t: (h, 0, 0)),
            pl.BlockSpec((None, RET_CHUNK, 1), lambda b, h, t: (h, 0, 0)),
            pl.BlockSpec((None, RET_CHUNK, 1), lambda b, h, t: (h, 0, 0)),
            pl.BlockSpec((None, 1, 1), lambda b, h, t: (h, 0, 0)),
        ],
        out_specs=pl.BlockSpec((RET_TS, RET_HEAD_V), lambda b, h, t: (b * steps + t, h)),
        scratch_shapes=[pltpu.VMEM((RET_HEAD_QK, RET_HEAD_V), jnp.float32)],
        compiler_params=_params("parallel", "parallel", "arbitrary"),
        name="ret_scan",
    )(proj, proj, proj, proj, intra, q_dec, k_dec, chunk_dec)


def _out_kernel(z_ref, w_ref, x_ref, gate_ref, o_ref, wb_ref):
    @pl.when(pl.program_id(1) == 0)
    def _():
        wb_ref[...] = w_ref[...].astype(jnp.bfloat16)

    y = jnp.dot(z_ref[...], wb_ref[...], preferred_element_type=jnp.float32)
    o_ref[...] = x_ref[...] + gate_ref[0] * y


def _ret_output(z, x2d, mod, mod_idx, seq, w_out, layer):
    n_tok = x2d.shape[0]
    tiles_per_seq = seq // OUT_TM
    gate_off = 2 * D_MODEL // OUT_TN
    return pl.pallas_call(
        _out_kernel,
        out_shape=jax.ShapeDtypeStruct(x2d.shape, jnp.float32),
        grid=(D_MODEL // OUT_TN, n_tok // OUT_TM),
        in_specs=[
            pl.BlockSpec((OUT_TM, RET_V_DIM), lambda j, i: (i, 0)),
            pl.BlockSpec((None, RET_V_DIM, OUT_TN), lambda j, i: (layer, 0, j)),
            pl.BlockSpec((OUT_TM, OUT_TN), lambda j, i: (i, j)),
            pl.BlockSpec((1, 1, OUT_TN), lambda j, i: (mod_idx + i // tiles_per_seq, 0, gate_off + j)),
        ],
        out_specs=pl.BlockSpec((OUT_TM, OUT_TN), lambda j, i: (i, j)),
        scratch_shapes=[pltpu.VMEM((RET_V_DIM, OUT_TN), jnp.bfloat16)],
        compiler_params=_params("parallel", "arbitrary"),
        name="ret_output",
    )(z, w_out, x2d, mod)


def _rotary_tables(seq):
    inv = ROPE_BASE ** (-np.arange(0, RET_HEAD_QK, 2, dtype=np.float64) / RET_HEAD_QK)
    ang = np.arange(seq, dtype=np.float64)[:, None] * inv[None, :]
    cos = np.repeat(np.cos(ang), 2, axis=1)
    sin = np.stack([-np.sin(ang), np.sin(ang)], axis=-1).reshape(seq, RET_HEAD_QK)
    return cos.astype(np.float32), sin.astype(np.float32)


def _decay_tables():
    gamma = 1.0 - 2.0 ** (-5.0 - np.arange(RET_HEADS, dtype=np.float64))
    log_g = np.log(gamma)
    j = np.arange(RET_CHUNK, dtype=np.float64)
    diff = j[:, None] - j[None, :]
    intra = np.where(diff[None] >= 0, np.exp(np.maximum(diff, 0.0)[None] * log_g[:, None, None]), 0.0)
    q_dec = np.exp((j[None, :] + 1.0) * log_g[:, None])[:, :, None]
    k_dec = np.exp((RET_CHUNK - 1.0 - j[None, :]) * log_g[:, None])[:, :, None]
    chunk_dec = np.exp(RET_CHUNK * log_g)[:, None, None]
    return tuple(t.astype(np.float32) for t in (intra, q_dec, k_dec, chunk_dec))


@jax.jit
def kernel(x, c, ada_w, ada_b, norm_g, pool_w, pool_scale, ret_w_in, ret_w_out, mlp_w1, mlp_w2, final_g):
    batch, seq, d = x.shape
    assert d == D_MODEL
    assert seq % max(POOL_TM, MLP_TM, PROJ_TM, OUT_TM, RET_TS) == 0

    ada_w = ada_w.reshape(DEPTH * 2, d, 3 * d)
    ada_b = ada_b.reshape(DEPTH * 2, 1, 3 * d)
    c_lanes = jnp.broadcast_to(c[:, :, None], (batch, d, LANES))
    mod, cond = _ada_first_layer(c_lanes, ada_w, ada_b)

    cos, sin = _rotary_tables(seq)
    intra, q_dec, k_dec, chunk_dec = _decay_tables()
    norm_rows = norm_g.reshape(DEPTH * 2, 1, d)
    scale_rows = pool_scale.reshape(-1, 1, d)

    x2d = x.reshape(batch * seq, d)
    for i in range(DEPTH):
        mod = mod.reshape(2 * batch, 1, 3 * d)
        mix_idx, mlp_idx = 0, batch
        if i % 2 == 0:
            x2d = _pool_layer(x2d, mod, mix_idx, seq, norm_rows, 2 * i, pool_w, scale_rows, i // 2)
        else:
            proj = _ret_projection(x2d, mod, mix_idx, seq, norm_rows, 2 * i, ret_w_in, i // 2, cos, sin)
            z = _ret_scan(proj, batch, seq, intra, q_dec, k_dec, chunk_dec)
            x2d = _ret_output(z, x2d, mod, mix_idx, seq, ret_w_out, i // 2)
        if i < DEPTH - 1:
            x2d, mod = _mlp_layer(x2d, mod, mlp_idx, seq, norm_rows, 2 * i + 1, mlp_w1, mlp_w2, i,
                                  ada=(ada_w, ada_b, cond))
        else:
            x2d = _mlp_layer(x2d, mod, mlp_idx, seq, norm_rows, 2 * i + 1, mlp_w1, mlp_w2, i,
                             final_g=final_g.reshape(1, d))
    return x2d.reshape(batch, seq, d)
```

```python
import functools

import jax
import jax.numpy as jnp
import numpy as np
from jax import lax
from jax.experimental import pallas as pl
from jax.experimental.pallas import tpu as pltpu

D_MODEL = 2048
DEPTH = 4
POOL_WINDOWS = (2, 4, 8, 16)
POOL_GROUP = D_MODEL // len(POOL_WINDOWS)
POOL_HALO = 16
RET_HEADS = 8
RET_QK_DIM = D_MODEL
RET_V_DIM = 2 * D_MODEL
RET_HEAD_QK = RET_QK_DIM // RET_HEADS
RET_HEAD_V = RET_V_DIM // RET_HEADS
RET_PROJ = 2 * RET_QK_DIM + 2 * RET_V_DIM
ROPE_BASE = 10000.0
D_FF = 4 * D_MODEL
EPS = 1e-6
LANES = 128

V7X_VMEM_BYTES = 64 * 1024 * 1024
VMEM_LIMIT_BYTES = V7X_VMEM_BYTES - 4 * 1024 * 1024

ROW_CHUNK = 128
ADA_TK = 1024
ADA_FUSED_TK = 128
POOL_TM = 512
MLP_TM = 1024
MLP_TF = 512
MLP_TN = 512
PROJ_TM = 1024
PROJ_TN = 1024
RET_TS = 1024
RET_CHUNK = 256
OUT_TM = 1024
OUT_TN = 512


def _params(*semantics):
    return pltpu.CompilerParams(dimension_semantics=semantics, vmem_limit_bytes=VMEM_LIMIT_BYTES)


def _norm_modulate(x, g, mod_row):
    shift = mod_row[:, 0:D_MODEL]
    scale = mod_row[:, D_MODEL:2 * D_MODEL]
    inv = lax.rsqrt(jnp.mean(x * x, axis=-1, keepdims=True) + EPS)
    return x * inv * (g * (1.0 + scale)) + shift


def _ada_columns(w, cond_ref, bias):
    rows = []
    for b in range(cond_ref.shape[0]):
        cond = cond_ref[b]
        rows.append(jnp.concatenate(
            [jnp.sum(w[:, lo:lo + LANES] * cond, axis=0, keepdims=True) for lo in range(0, w.shape[1], LANES)],
            axis=1))
    return jnp.concatenate(rows, axis=0) + bias


def _ada_kernel(c_ref, w_ref, b_ref, o_ref, cond_ref):
    cond_ref[...] = jax.nn.silu(c_ref[...])
    o_ref[...] = _ada_columns(w_ref[...], cond_ref, b_ref[...])


def _ada_first_layer(c_lanes, ada_w, ada_b):
    batch = c_lanes.shape[0]
    n_out = ada_w.shape[2]
    return pl.pallas_call(
        _ada_kernel,
        out_shape=(jax.ShapeDtypeStruct((2, batch, n_out), jnp.float32),
                   jax.ShapeDtypeStruct(c_lanes.shape, jnp.float32)),
        grid=(2, n_out // ADA_TK),
        in_specs=[
            pl.BlockSpec(c_lanes.shape, lambda l, j: (0, 0, 0)),
            pl.BlockSpec((None, D_MODEL, ADA_TK), lambda l, j: (l, 0, j)),
            pl.BlockSpec((None, 1, ADA_TK), lambda l, j: (l, 0, j)),
        ],
        out_specs=(pl.BlockSpec((None, batch, ADA_TK), lambda l, j: (l, 0, j)),
                   pl.BlockSpec(c_lanes.shape, lambda l, j: (0, 0, 0))),
        compiler_params=_params("arbitrary", "arbitrary"),
        name="ada_modulation",
    )(c_lanes, ada_w, ada_b)


def _pool_kernel(tiles_per_seq, x_ref, halo_ref, mod_ref, g_ref, w_ref, ls_ref, o_ref, wb_ref):
    i = pl.program_id(0)

    @pl.when(i == 0)
    def _():
        wb_ref[...] = w_ref[...].astype(jnp.bfloat16)

    tile_in_seq = i % tiles_per_seq
    tm = x_ref.shape[0]
    mod_row = mod_ref[0]
    g = g_ref[0]
    x = x_ref[...]
    h = _norm_modulate(x, g, mod_row)
    h_halo = _norm_modulate(halo_ref[...], g, mod_row)
    h_halo = jnp.where(tile_in_seq == 0, 0.0, h_halo)
    out_scale = mod_row[:, 2 * D_MODEL:3 * D_MODEL] * ls_ref[0]
    pos = tile_in_seq * tm + lax.broadcasted_iota(jnp.int32, (tm, 1), 0)
    for gi, window in enumerate(POOL_WINDOWS):
        cols = slice(gi * POOL_GROUP, (gi + 1) * POOL_GROUP)
        hg = h[:, cols]
        acc = jnp.concatenate([h_halo[:, cols], hg], axis=0)
        shift = 1
        while shift < window:
            acc = acc + pltpu.roll(acc, shift, axis=0)
            shift *= 2
        cnt = jnp.minimum(pos + 1, window).astype(jnp.float32)
        p = acc[POOL_HALO:, :] / cnt - hg
        y = jnp.dot(p.astype(jnp.bfloat16), wb_ref[gi], preferred_element_type=jnp.float32)
        o_ref[:, cols] = x[:, cols] + out_scale[:, cols] * y


def _pool_layer(x2d, mod, mod_idx, seq, norm_g, norm_idx, pool_w, pool_scale, pool_idx):
    n_tok = x2d.shape[0]
    tiles_per_seq = seq // POOL_TM
    halo_blocks_per_tile = POOL_TM // POOL_HALO
    n_grp = len(POOL_WINDOWS)
    return pl.pallas_call(
        functools.partial(_pool_kernel, tiles_per_seq),
        out_shape=jax.ShapeDtypeStruct(x2d.shape, jnp.float32),
        grid=(n_tok // POOL_TM,),
        in_specs=[
            pl.BlockSpec((POOL_TM, D_MODEL), lambda i: (i, 0)),
            pl.BlockSpec((POOL_HALO, D_MODEL), lambda i: (jnp.maximum(i * halo_blocks_per_tile - 1, 0), 0)),
            pl.BlockSpec((1, 1, 3 * D_MODEL), lambda i: (mod_idx + i // tiles_per_seq, 0, 0)),
            pl.BlockSpec((1, 1, D_MODEL), lambda i: (norm_idx, 0, 0)),
            pl.BlockSpec((None, n_grp, POOL_GROUP, POOL_GROUP), lambda i: (pool_idx, 0, 0, 0)),
            pl.BlockSpec((1, 1, D_MODEL), lambda i: (pool_idx, 0, 0)),
        ],
        out_specs=pl.BlockSpec((POOL_TM, D_MODEL), lambda i: (i, 0)),
        scratch_shapes=[pltpu.VMEM((n_grp, POOL_GROUP, POOL_GROUP), jnp.bfloat16)],
        compiler_params=_params("arbitrary"),
        name="pool_mixer",
    )(x2d, x2d, mod, norm_g, pool_w, pool_scale)


def _row_chunk_copies(x_hbm, first_row, dst_of_chunk, sems):
    return [
        pltpu.make_async_copy(x_hbm.at[pl.ds(first_row + k * ROW_CHUNK, ROW_CHUNK), :], dst_of_chunk(k), sems.at[k])
        for k in range(sems.shape[0])
    ]


def _mlp_kernel(last, x_hbm, mod_ref, g_ref, w1_ref, w2_ref, *refs):
    if last:
        fg_ref, o_ref, h_ref, sems = refs
    else:
        wa_ref, ba_ref, cond_ref, o_ref, next_mod_ref, h_ref, sems = refs
    j = pl.program_id(1)

    @pl.when(j == 0)
    def _():
        first_row = pl.multiple_of(pl.program_id(0) * MLP_TM, MLP_TM)
        copies = _row_chunk_copies(x_hbm, first_row, lambda k: o_ref.at[pl.ds(k * ROW_CHUNK, ROW_CHUNK), :], sems)
        for cp in copies:
            cp.start()
        for k, cp in enumerate(copies):
            cp.wait()
            rows = pl.ds(k * ROW_CHUNK, ROW_CHUNK)
            h_ref[rows, :] = _norm_modulate(o_ref[rows, :], g_ref[0], mod_ref[0]).astype(jnp.bfloat16)

    if not last:
        next_mod_ref[...] = _ada_columns(wa_ref[...], cond_ref, ba_ref[...])

    a = jnp.dot(h_ref[...], w1_ref[...].astype(jnp.bfloat16), preferred_element_type=jnp.float32)
    a = jnp.maximum(a, 0.0)
    a = (a * a).astype(jnp.bfloat16)
    w2 = w2_ref[...].astype(jnp.bfloat16)
    for n in range(0, D_MODEL, MLP_TN):
        gate = mod_ref[0, :, 2 * D_MODEL + n:2 * D_MODEL + n + MLP_TN]
        o_ref[:, n:n + MLP_TN] += gate * jnp.dot(a, w2[:, n:n + MLP_TN], preferred_element_type=jnp.float32)

    if last:
        @pl.when(j == pl.num_programs(1) - 1)
        def _():
            y = o_ref[...]
            inv = lax.rsqrt(jnp.mean(y * y, axis=-1, keepdims=True) + EPS)
            o_ref[...] = y * inv * fg_ref[...]


def _mlp_layer(x2d, mod, mod_idx, seq, norm_g, norm_idx, w1, w2, layer, final_g=None, ada=None):
    n_tok = x2d.shape[0]
    tiles_per_seq = seq // MLP_TM
    steps = D_FF // MLP_TF
    last = ada is None
    in_specs = [
        pl.BlockSpec(memory_space=pl.ANY),
        pl.BlockSpec((1, 1, 3 * D_MODEL), lambda i, j: (mod_idx + i // tiles_per_seq, 0, 0)),
        pl.BlockSpec((1, 1, D_MODEL), lambda i, j: (norm_idx, 0, 0)),
        pl.BlockSpec((None, D_MODEL, MLP_TF), lambda i, j: (layer, 0, j)),
        pl.BlockSpec((None, MLP_TF, D_MODEL), lambda i, j: (layer, j, 0)),
    ]
    x_spec = pl.BlockSpec((MLP_TM, D_MODEL), lambda i, j: (i, 0))
    x_shape = jax.ShapeDtypeStruct(x2d.shape, jnp.float32)
    if last:
        operands = (x2d, mod, norm_g, w1, w2, final_g)
        in_specs.append(pl.BlockSpec((1, D_MODEL), lambda i, j: (0, 0)))
        out_specs, out_shape = x_spec, x_shape
    else:
        ada_w, ada_b, cond = operands_ada = ada
        batch = cond.shape[0]
        blocks_per_row = 3 * D_MODEL // ADA_FUSED_TK
        n_blocks = 2 * blocks_per_row
        assert n_blocks <= (n_tok // MLP_TM) * steps

        def ada_block(i, j):
            blk = jnp.minimum(i * steps + j, n_blocks - 1)
            return 2 * (layer + 1) + blk // blocks_per_row, 0, blk % blocks_per_row

        def next_mod_block(i, j):
            row, _, col = ada_block(i, j)
            return row - 2 * (layer + 1), 0, col

        operands = (x2d, mod, norm_g, w1, w2) + operands_ada
        in_specs += [
            pl.BlockSpec((None, D_MODEL, ADA_FUSED_TK), ada_block),
            pl.BlockSpec((None, 1, ADA_FUSED_TK), ada_block),
            pl.BlockSpec(cond.shape, lambda i, j: (0, 0, 0)),
        ]
        out_specs = (x_spec, pl.BlockSpec((None, batch, ADA_FUSED_TK), next_mod_block))
        out_shape = (x_shape, jax.ShapeDtypeStruct((2, batch, 3 * D_MODEL), jnp.float32))
    return pl.pallas_call(
        functools.partial(_mlp_kernel, last),
        out_shape=out_shape,
        grid=(n_tok // MLP_TM, steps),
        in_specs=in_specs,
        out_specs=out_specs,
        scratch_shapes=[pltpu.VMEM((MLP_TM, D_MODEL), jnp.bfloat16),
                        pltpu.SemaphoreType.DMA((MLP_TM // ROW_CHUNK,))],
        compiler_params=_params("arbitrary", "arbitrary"),
        name="mlp_final" if last else "mlp",
    )(*operands)


def _proj_kernel(x_hbm, mod_ref, g_ref, w_ref, cos_ref, sin_ref, o_ref, h_ref, xbuf_ref, sems):
    j = pl.program_id(1)
    n_q = RET_QK_DIM // PROJ_TN

    @pl.when(j == 0)
    def _():
        first_row = pl.multiple_of(pl.program_id(0) * PROJ_TM, PROJ_TM)
        copies = _row_chunk_copies(x_hbm, first_row, lambda k: xbuf_ref.at[k], sems)
        for cp in copies:
            cp.start()
        for k, cp in enumerate(copies):
            cp.wait()
            rows = pl.ds(k * ROW_CHUNK, ROW_CHUNK)
            h_ref[rows, :] = _norm_modulate(xbuf_ref[k], g_ref[0], mod_ref[0]).astype(jnp.bfloat16)

    @pl.when(j < 2 * n_q)
    def _():
        k_scale = jnp.where(j >= n_q, RET_HEAD_QK ** -0.5, 1.0)
        cos = cos_ref[...] * k_scale
        sin = sin_ref[...] * k_scale
        swap = lax.broadcasted_iota(jnp.int32, (PROJ_TM, LANES), 1) ^ 1
        r = jnp.dot(h_ref[...], w_ref[...].astype(jnp.bfloat16), preferred_element_type=jnp.float32)
        for lo in range(0, PROJ_TN, LANES):
            t = lo % RET_HEAD_QK
            xs = r[:, lo:lo + LANES]
            partner = jnp.take_along_axis(xs, swap, axis=1)
            o_ref[:, lo:lo + LANES] = (xs * cos[:, t:t + LANES] + partner * sin[:, t:t + LANES]).astype(o_ref.dtype)

    @pl.when(j >= 2 * n_q)
    def _():
        w = w_ref[...].astype(jnp.bfloat16)
        o_ref[...] = jnp.dot(h_ref[...], w, preferred_element_type=jnp.float32).astype(o_ref.dtype)


def _ret_projection(x2d, mod, mod_idx, seq, norm_g, norm_idx, w_in, layer, cos, sin):
    n_tok = x2d.shape[0]
    tiles_per_seq = seq // PROJ_TM
    return pl.pallas_call(
        _proj_kernel,
        out_shape=jax.ShapeDtypeStruct((n_tok, RET_PROJ), jnp.bfloat16),
        grid=(n_tok // PROJ_TM, RET_PROJ // PROJ_TN),
        in_specs=[
            pl.BlockSpec(memory_space=pl.ANY),
            pl.BlockSpec((1, 1, 3 * D_MODEL), lambda i, j: (mod_idx + i // tiles_per_seq, 0, 0)),
            pl.BlockSpec((1, 1, D_MODEL), lambda i, j: (norm_idx, 0, 0)),
            pl.BlockSpec((None, D_MODEL, PROJ_TN), lambda i, j: (layer, 0, j)),
            pl.BlockSpec((PROJ_TM, RET_HEAD_QK), lambda i, j: (i % tiles_per_seq, 0)),
            pl.BlockSpec((PROJ_TM, RET_HEAD_QK), lambda i, j: (i % tiles_per_seq, 0)),
        ],
        out_specs=pl.BlockSpec((PROJ_TM, PROJ_TN), lambda i, j: (i, j)),
        scratch_shapes=[pltpu.VMEM((PROJ_TM, D_MODEL), jnp.bfloat16),
                        pltpu.VMEM((PROJ_TM // ROW_CHUNK, ROW_CHUNK, D_MODEL), jnp.float32),
                        pltpu.SemaphoreType.DMA((PROJ_TM // ROW_CHUNK,))],
        compiler_params=_params("parallel", "arbitrary"),
        name="ret_projection",
    )(x2d, mod, norm_g, w_in, cos, sin)


def _scan_kernel(q_ref, k_ref, v_ref, g_ref, mask_ref, eps_ref, kdec_ref, cdec_ref, o_ref, state_ref):
    @pl.when(pl.program_id(2) == 0)
    def _():
        state_ref[...] = jnp.zeros_like(state_ref)

    mask = mask_ref[...]
    row_eps = eps_ref[...]
    k_dec = kdec_ref[...]
    chunk_dec = cdec_ref[...]
    state = state_ref[...]
    for c in range(q_ref.shape[0] // RET_CHUNK):
        rows = slice(c * RET_CHUNK, (c + 1) * RET_CHUNK)
        qc = q_ref[rows, :]
        kc = k_ref[rows, :]
        vc = v_ref[rows, :]
        scores = lax.dot_general(qc, kc, (((1,), (1,)), ((), ())), preferred_element_type=jnp.float32)
        scores = (scores * mask).astype(jnp.bfloat16)
        y = (jnp.dot(scores, vc, preferred_element_type=jnp.float32)
             + jnp.dot(qc, state.astype(jnp.bfloat16), preferred_element_type=jnp.float32))
        k_decayed = (kc.astype(jnp.float32) * k_dec).astype(jnp.bfloat16)
        state = chunk_dec * state + lax.dot_general(
            k_decayed, vc, (((0,), (0,)), ((), ())), preferred_element_type=jnp.float32)
        mu = jnp.mean(y, axis=-1, keepdims=True)
        yc = y - mu
        var = jnp.mean(yc * yc, axis=-1, keepdims=True)
        yn = yc * lax.rsqrt(var + row_eps)
        o_ref[rows, :] = (jax.nn.silu(g_ref[rows, :].astype(jnp.float32)) * yn).astype(o_ref.dtype)
    state_ref[...] = state


def _ret_scan(proj, batch, seq, mask, row_eps, k_dec, chunk_dec):
    n_tok = proj.shape[0]
    steps = seq // RET_TS
    k_off = RET_QK_DIM // RET_HEAD_QK
    v_off = 2 * RET_QK_DIM // RET_HEAD_V
    g_off = v_off + RET_HEADS
    return pl.pallas_call(
        _scan_kernel,
        out_shape=jax.ShapeDtypeStruct((n_tok, RET_V_DIM), jnp.bfloat16),
        grid=(batch, RET_HEADS, steps),
        in_specs=[
            pl.BlockSpec((RET_TS, RET_HEAD_QK), lambda b, h, t: (b * steps + t, h)),
            pl.BlockSpec((RET_TS, RET_HEAD_QK), lambda b, h, t: (b * steps + t, k_off + h)),
            pl.BlockSpec((RET_TS, RET_HEAD_V), lambda b, h, t: (b * steps + t, v_off + h)),
            pl.BlockSpec((RET_TS, RET_HEAD_V), lambda b, h, t: (b * steps + t, g_off + h)),
            pl.BlockSpec((None, RET_CHUNK, RET_CHUNK), lambda b, h, t: (h, 0, 0)),
            pl.BlockSpec((None, RET_CHUNK, 1), lambda b, h, t: (h, 0, 0)),
            pl.BlockSpec((None, RET_CHUNK, 1), lambda b, h, t: (h, 0, 0)),
            pl.BlockSpec((None, 1, 1), lambda b, h, t: (h, 0, 0)),
        ],
        out_specs=pl.BlockSpec((RET_TS, RET_HEAD_V), lambda b, h, t: (b * steps + t, h)),
        scratch_shapes=[pltpu.VMEM((RET_HEAD_QK, RET_HEAD_V), jnp.float32)],
        compiler_params=_params("parallel", "parallel", "arbitrary"),
        name="ret_scan",
    )(proj, proj, proj, proj, mask, row_eps, k_dec, chunk_dec)


def _out_kernel(z_ref, w_ref, x_ref, gate_ref, o_ref, wb_ref):
    @pl.when(pl.program_id(1) == 0)
    def _():
        wb_ref[...] = w_ref[...].astype(jnp.bfloat16)

    y = jnp.dot(z_ref[...], wb_ref[...], preferred_element_type=jnp.float32)
    o_ref[...] = x_ref[...] + gate_ref[0] * y


def _ret_output(z, x2d, mod, mod_idx, seq, w_out, layer):
    n_tok = x2d.shape[0]
    tiles_per_seq = seq // OUT_TM
    gate_off = 2 * D_MODEL // OUT_TN
    return pl.pallas_call(
        _out_kernel,
        out_shape=jax.ShapeDtypeStruct(x2d.shape, jnp.float32),
        grid=(D_MODEL // OUT_TN, n_tok // OUT_TM),
        in_specs=[
            pl.BlockSpec((OUT_TM, RET_V_DIM), lambda j, i: (i, 0)),
            pl.BlockSpec((None, RET_V_DIM, OUT_TN), lambda j, i: (layer, 0, j)),
            pl.BlockSpec((OUT_TM, OUT_TN), lambda j, i: (i, j)),
            pl.BlockSpec((1, 1, OUT_TN), lambda j, i: (mod_idx + i // tiles_per_seq, 0, gate_off + j)),
        ],
        out_specs=pl.BlockSpec((OUT_TM, OUT_TN), lambda j, i: (i, j)),
        scratch_shapes=[pltpu.VMEM((RET_V_DIM, OUT_TN), jnp.bfloat16)],
        compiler_params=_params("parallel", "arbitrary"),
        name="ret_output",
    )(z, w_out, x2d, mod)


def _rotary_tables(seq):
    inv = ROPE_BASE ** (-np.arange(0, RET_HEAD_QK, 2, dtype=np.float64) / RET_HEAD_QK)
    ang = np.arange(seq, dtype=np.float64)[:, None] * inv[None, :]
    cos = np.repeat(np.cos(ang), 2, axis=1)
    sin = np.stack([-np.sin(ang), np.sin(ang)], axis=-1).reshape(seq, RET_HEAD_QK)
    return cos.astype(np.float32), sin.astype(np.float32)


def _decay_tables():
    gamma = 1.0 - 2.0 ** (-5.0 - np.arange(RET_HEADS, dtype=np.float64))
    log_g = np.log(gamma)
    j = np.arange(RET_CHUNK, dtype=np.float64)
    causal = j[:, None] >= j[None, :]
    mask = np.where(causal[None], np.exp(-(j[None, None, :] + 1.0) * log_g[:, None, None]), 0.0)
    row_eps = (EPS * np.exp(-2.0 * (j[None, :] + 1.0) * log_g[:, None]))[:, :, None]
    k_dec = np.exp((RET_CHUNK - 1.0 - j[None, :]) * log_g[:, None])[:, :, None]
    chunk_dec = np.exp(RET_CHUNK * log_g)[:, None, None]
    return tuple(t.astype(np.float32) for t in (mask, row_eps, k_dec, chunk_dec))


@jax.jit
def kernel(x, c, ada_w, ada_b, norm_g, pool_w, pool_scale, ret_w_in, ret_w_out, mlp_w1, mlp_w2, final_g):
    batch, seq, d = x.shape
    assert d == D_MODEL
    assert seq % max(POOL_TM, MLP_TM, PROJ_TM, OUT_TM, RET_TS) == 0

    ada_w = ada_w.reshape(DEPTH * 2, d, 3 * d)
    ada_b = ada_b.reshape(DEPTH * 2, 1, 3 * d)
    c_lanes = jnp.broadcast_to(c[:, :, None], (batch, d, LANES))
    mod, cond = _ada_first_layer(c_lanes, ada_w, ada_b)

    cos, sin = _rotary_tables(seq)
    mask, row_eps, k_dec, chunk_dec = _decay_tables()
    norm_rows = norm_g.reshape(DEPTH * 2, 1, d)
    scale_rows = pool_scale.reshape(-1, 1, d)

    x2d = x.reshape(batch * seq, d)
    for i in range(DEPTH):
        mod = mod.reshape(2 * batch, 1, 3 * d)
        mix_idx, mlp_idx = 0, batch
        if i % 2 == 0:
            x2d = _pool_layer(x2d, mod, mix_idx, seq, norm_rows, 2 * i, pool_w, scale_rows, i // 2)
        else:
            proj = _ret_projection(x2d, mod, mix_idx, seq, norm_rows, 2 * i, ret_w_in, i // 2, cos, sin)
            z = _ret_scan(proj, batch, seq, mask, row_eps, k_dec, chunk_dec)
            x2d = _ret_output(z, x2d, mod, mix_idx, seq, ret_w_out, i // 2)
        if i < DEPTH - 1:
            x2d, mod = _mlp_layer(x2d, mod, mlp_idx, seq, norm_rows, 2 * i + 1, mlp_w1, mlp_w2, i,
                                  ada=(ada_w, ada_b, cond))
        else:
            x2d = _mlp_layer(x2d, mod, mlp_idx, seq, norm_rows, 2 * i + 1, mlp_w1, mlp_w2, i,
                             final_g=final_g.reshape(1, d))
    return x2d.reshape(batch, seq, d)
```

```python
import functools

import jax
import jax.numpy as jnp
import numpy as np
from jax import lax
from jax.experimental import pallas as pl
from jax.experimental.pallas import tpu as pltpu

D_MODEL = 2048
DEPTH = 4
POOL_WINDOWS = (2, 4, 8, 16)
POOL_GROUP = D_MODEL // len(POOL_WINDOWS)
POOL_HALO = 16
RET_HEADS = 8
RET_QK_DIM = D_MODEL
RET_V_DIM = 2 * D_MODEL
RET_HEAD_QK = RET_QK_DIM // RET_HEADS
RET_HEAD_V = RET_V_DIM // RET_HEADS
RET_PROJ = 2 * RET_QK_DIM + 2 * RET_V_DIM
ROPE_BASE = 10000.0
D_FF = 4 * D_MODEL
EPS = 1e-6
LANES = 128

V7X_VMEM_BYTES = 64 * 1024 * 1024
VMEM_LIMIT_BYTES = V7X_VMEM_BYTES - 4 * 1024 * 1024

ROW_CHUNK = 128
ADA_TK = 1024
ADA_FUSED_TK = 128
POOL_TM = 512
MLP_TM = 1024
MLP_TF = 512
MLP_TN = 512
PROJ_TM = 1024
PROJ_TN = 1024
PROJ_GROUP = 256
RET_TS = 1024
RET_CHUNK = 256
OUT_TM = 1024
OUT_TN = 512


def _params(*semantics):
    return pltpu.CompilerParams(dimension_semantics=semantics, vmem_limit_bytes=VMEM_LIMIT_BYTES)


def _norm_modulate(x, g, mod_row):
    shift = mod_row[:, 0:D_MODEL]
    scale = mod_row[:, D_MODEL:2 * D_MODEL]
    inv = lax.rsqrt(jnp.mean(x * x, axis=-1, keepdims=True) + EPS)
    return x * inv * (g * (1.0 + scale)) + shift


def _ada_columns(w, cond_ref, bias):
    rows = []
    for b in range(cond_ref.shape[0]):
        cond = cond_ref[b]
        rows.append(jnp.concatenate(
            [jnp.sum(w[:, lo:lo + LANES] * cond, axis=0, keepdims=True) for lo in range(0, w.shape[1], LANES)],
            axis=1))
    return jnp.concatenate(rows, axis=0) + bias


def _ada_kernel(c_ref, w_ref, b_ref, o_ref, cond_ref):
    cond_ref[...] = jax.nn.silu(c_ref[...])
    o_ref[...] = _ada_columns(w_ref[...], cond_ref, b_ref[...])


def _ada_first_layer(c_lanes, ada_w, ada_b):
    batch = c_lanes.shape[0]
    n_out = ada_w.shape[2]
    return pl.pallas_call(
        _ada_kernel,
        out_shape=(jax.ShapeDtypeStruct((2, batch, n_out), jnp.float32),
                   jax.ShapeDtypeStruct(c_lanes.shape, jnp.float32)),
        grid=(2, n_out // ADA_TK),
        in_specs=[
            pl.BlockSpec(c_lanes.shape, lambda l, j: (0, 0, 0)),
            pl.BlockSpec((None, D_MODEL, ADA_TK), lambda l, j: (l, 0, j)),
            pl.BlockSpec((None, 1, ADA_TK), lambda l, j: (l, 0, j)),
        ],
        out_specs=(pl.BlockSpec((None, batch, ADA_TK), lambda l, j: (l, 0, j)),
                   pl.BlockSpec(c_lanes.shape, lambda l, j: (0, 0, 0))),
        compiler_params=_params("arbitrary", "arbitrary"),
        name="ada_modulation",
    )(c_lanes, ada_w, ada_b)


def _pool_kernel(tiles_per_seq, x_ref, halo_ref, mod_ref, g_ref, w_ref, ls_ref, o_ref, wb_ref):
    i = pl.program_id(0)

    @pl.when(i == 0)
    def _():
        wb_ref[...] = w_ref[...].astype(jnp.bfloat16)

    tile_in_seq = i % tiles_per_seq
    tm = x_ref.shape[0]
    mod_row = mod_ref[0]
    g = g_ref[0]
    x = x_ref[...]
    h = _norm_modulate(x, g, mod_row)
    h_halo = _norm_modulate(halo_ref[...], g, mod_row)
    h_halo = jnp.where(tile_in_seq == 0, 0.0, h_halo)
    out_scale = mod_row[:, 2 * D_MODEL:3 * D_MODEL] * ls_ref[0]
    pos = tile_in_seq * tm + lax.broadcasted_iota(jnp.int32, (tm, 1), 0)
    for gi, window in enumerate(POOL_WINDOWS):
        cols = slice(gi * POOL_GROUP, (gi + 1) * POOL_GROUP)
        hg = h[:, cols]
        acc = jnp.concatenate([h_halo[:, cols], hg], axis=0)
        shift = 1
        while shift < window:
            acc = acc + pltpu.roll(acc, shift, axis=0)
            shift *= 2
        cnt = jnp.minimum(pos + 1, window).astype(jnp.float32)
        p = acc[POOL_HALO:, :] / cnt - hg
        y = jnp.dot(p.astype(jnp.bfloat16), wb_ref[gi], preferred_element_type=jnp.float32)
        o_ref[:, cols] = x[:, cols] + out_scale[:, cols] * y


def _pool_layer(x2d, mod, mod_idx, seq, norm_g, norm_idx, pool_w, pool_scale, pool_idx):
    n_tok = x2d.shape[0]
    tiles_per_seq = seq // POOL_TM
    halo_blocks_per_tile = POOL_TM // POOL_HALO
    n_grp = len(POOL_WINDOWS)
    return pl.pallas_call(
        functools.partial(_pool_kernel, tiles_per_seq),
        out_shape=jax.ShapeDtypeStruct(x2d.shape, jnp.float32),
        grid=(n_tok // POOL_TM,),
        in_specs=[
            pl.BlockSpec((POOL_TM, D_MODEL), lambda i: (i, 0)),
            pl.BlockSpec((POOL_HALO, D_MODEL), lambda i: (jnp.maximum(i * halo_blocks_per_tile - 1, 0), 0)),
            pl.BlockSpec((1, 1, 3 * D_MODEL), lambda i: (mod_idx + i // tiles_per_seq, 0, 0)),
            pl.BlockSpec((1, 1, D_MODEL), lambda i: (norm_idx, 0, 0)),
            pl.BlockSpec((None, n_grp, POOL_GROUP, POOL_GROUP), lambda i: (pool_idx, 0, 0, 0)),
            pl.BlockSpec((1, 1, D_MODEL), lambda i: (pool_idx, 0, 0)),
        ],
        out_specs=pl.BlockSpec((POOL_TM, D_MODEL), lambda i: (i, 0)),
        scratch_shapes=[pltpu.VMEM((n_grp, POOL_GROUP, POOL_GROUP), jnp.bfloat16)],
        compiler_params=_params("arbitrary"),
        name="pool_mixer",
    )(x2d, x2d, mod, norm_g, pool_w, pool_scale)


def _row_chunk_copies(x_hbm, first_row, dst_of_chunk, sems):
    return [
        pltpu.make_async_copy(x_hbm.at[pl.ds(first_row + k * ROW_CHUNK, ROW_CHUNK), :], dst_of_chunk(k), sems.at[k])
        for k in range(sems.shape[0])
    ]


def _mlp_kernel(last, x_hbm, mod_ref, g_ref, w1_ref, w2_ref, *refs):
    if last:
        fg_ref, o_ref, h_ref, sems = refs
    else:
        wa_ref, ba_ref, cond_ref, o_ref, next_mod_ref, h_ref, sems = refs
    j = pl.program_id(1)

    @pl.when(j == 0)
    def _():
        first_row = pl.multiple_of(pl.program_id(0) * MLP_TM, MLP_TM)
        copies = _row_chunk_copies(x_hbm, first_row, lambda k: o_ref.at[pl.ds(k * ROW_CHUNK, ROW_CHUNK), :], sems)
        for cp in copies:
            cp.start()
        for k, cp in enumerate(copies):
            cp.wait()
            rows = pl.ds(k * ROW_CHUNK, ROW_CHUNK)
            h_ref[rows, :] = _norm_modulate(o_ref[rows, :], g_ref[0], mod_ref[0]).astype(jnp.bfloat16)

    if not last:
        next_mod_ref[...] = _ada_columns(wa_ref[...], cond_ref, ba_ref[...])

    a = jnp.dot(h_ref[...], w1_ref[...].astype(jnp.bfloat16), preferred_element_type=jnp.float32)
    a = jnp.maximum(a, 0.0)
    a = (a * a).astype(jnp.bfloat16)
    w2 = w2_ref[...].astype(jnp.bfloat16)
    for n in range(0, D_MODEL, MLP_TN):
        gate = mod_ref[0, :, 2 * D_MODEL + n:2 * D_MODEL + n + MLP_TN]
        o_ref[:, n:n + MLP_TN] += gate * jnp.dot(a, w2[:, n:n + MLP_TN], preferred_element_type=jnp.float32)

    if last:
        @pl.when(j == pl.num_programs(1) - 1)
        def _():
            y = o_ref[...]
            inv = lax.rsqrt(jnp.mean(y * y, axis=-1, keepdims=True) + EPS)
            o_ref[...] = y * inv * fg_ref[...]


def _mlp_layer(x2d, mod, mod_idx, seq, norm_g, norm_idx, w1, w2, layer, final_g=None, ada=None):
    n_tok = x2d.shape[0]
    tiles_per_seq = seq // MLP_TM
    steps = D_FF // MLP_TF
    last = ada is None
    in_specs = [
        pl.BlockSpec(memory_space=pl.ANY),
        pl.BlockSpec((1, 1, 3 * D_MODEL), lambda i, j: (mod_idx + i // tiles_per_seq, 0, 0)),
        pl.BlockSpec((1, 1, D_MODEL), lambda i, j: (norm_idx, 0, 0)),
        pl.BlockSpec((None, D_MODEL, MLP_TF), lambda i, j: (layer, 0, j)),
        pl.BlockSpec((None, MLP_TF, D_MODEL), lambda i, j: (layer, j, 0)),
    ]
    x_spec = pl.BlockSpec((MLP_TM, D_MODEL), lambda i, j: (i, 0))
    x_shape = jax.ShapeDtypeStruct(x2d.shape, jnp.float32)
    if last:
        operands = (x2d, mod, norm_g, w1, w2, final_g)
        in_specs.append(pl.BlockSpec((1, D_MODEL), lambda i, j: (0, 0)))
        out_specs, out_shape = x_spec, x_shape
    else:
        ada_w, ada_b, cond = operands_ada = ada
        batch = cond.shape[0]
        blocks_per_row = 3 * D_MODEL // ADA_FUSED_TK
        n_blocks = 2 * blocks_per_row
        assert n_blocks <= (n_tok // MLP_TM) * steps

        def ada_block(i, j):
            blk = jnp.minimum(i * steps + j, n_blocks - 1)
            return 2 * (layer + 1) + blk // blocks_per_row, 0, blk % blocks_per_row

        def next_mod_block(i, j):
            row, _, col = ada_block(i, j)
            return row - 2 * (layer + 1), 0, col

        operands = (x2d, mod, norm_g, w1, w2) + operands_ada
        in_specs += [
            pl.BlockSpec((None, D_MODEL, ADA_FUSED_TK), ada_block),
            pl.BlockSpec((None, 1, ADA_FUSED_TK), ada_block),
            pl.BlockSpec(cond.shape, lambda i, j: (0, 0, 0)),
        ]
        out_specs = (x_spec, pl.BlockSpec((None, batch, ADA_FUSED_TK), next_mod_block))
        out_shape = (x_shape, jax.ShapeDtypeStruct((2, batch, 3 * D_MODEL), jnp.float32))
    return pl.pallas_call(
        functools.partial(_mlp_kernel, last),
        out_shape=out_shape,
        grid=(n_tok // MLP_TM, steps),
        in_specs=in_specs,
        out_specs=out_specs,
        scratch_shapes=[pltpu.VMEM((MLP_TM, D_MODEL), jnp.bfloat16),
                        pltpu.SemaphoreType.DMA((MLP_TM // ROW_CHUNK,))],
        compiler_params=_params("arbitrary", "arbitrary"),
        name="mlp_final" if last else "mlp",
    )(*operands)


def _proj_kernel(x_hbm, mod_ref, g_ref, w_ref, cos_ref, sin_ref, o_ref, h_ref, xbuf_ref, sems):
    j = pl.program_id(1)
    n_q = RET_QK_DIM // PROJ_TN

    @pl.when(j == 0)
    def _():
        first_row = pl.multiple_of(pl.program_id(0) * PROJ_TM, PROJ_TM)
        copies = _row_chunk_copies(x_hbm, first_row, lambda k: xbuf_ref.at[k], sems)
        for cp in copies:
            cp.start()
        for k, cp in enumerate(copies):
            cp.wait()
            rows = pl.ds(k * ROW_CHUNK, ROW_CHUNK)
            h_ref[rows, :] = _norm_modulate(xbuf_ref[k], g_ref[0], mod_ref[0]).astype(jnp.bfloat16)

    @pl.when(j < 2 * n_q)
    def _():
        k_scale = jnp.where(j >= n_q, RET_HEAD_QK ** -0.5, 1.0)
        cos = cos_ref[...] * k_scale
        sin = sin_ref[...] * k_scale
        swap = lax.broadcasted_iota(jnp.int32, (PROJ_TM, LANES), 1) ^ 1
        r = jnp.dot(h_ref[...], w_ref[...].astype(jnp.bfloat16), preferred_element_type=jnp.float32)
        for lo in range(0, PROJ_TN, LANES):
            t = lo % RET_HEAD_QK
            xs = r[:, lo:lo + LANES]
            partner = jnp.take_along_axis(xs, swap, axis=1)
            o_ref[lo // PROJ_GROUP, :, t:t + LANES] = (
                xs * cos[:, t:t + LANES] + partner * sin[:, t:t + LANES]).astype(o_ref.dtype)

    @pl.when(j >= 2 * n_q)
    def _():
        w = w_ref[...].astype(jnp.bfloat16)
        r = jnp.dot(h_ref[...], w, preferred_element_type=jnp.float32)
        for grp in range(PROJ_TN // PROJ_GROUP):
            o_ref[grp] = r[:, grp * PROJ_GROUP:(grp + 1) * PROJ_GROUP].astype(o_ref.dtype)


def _ret_projection(x2d, mod, mod_idx, seq, norm_g, norm_idx, w_in, layer, cos, sin):
    n_tok = x2d.shape[0]
    tiles_per_seq = seq // PROJ_TM
    return pl.pallas_call(
        _proj_kernel,
        out_shape=jax.ShapeDtypeStruct((RET_PROJ // PROJ_GROUP, n_tok, PROJ_GROUP), jnp.bfloat16),
        grid=(n_tok // PROJ_TM, RET_PROJ // PROJ_TN),
        in_specs=[
            pl.BlockSpec(memory_space=pl.ANY),
            pl.BlockSpec((1, 1, 3 * D_MODEL), lambda i, j: (mod_idx + i // tiles_per_seq, 0, 0)),
            pl.BlockSpec((1, 1, D_MODEL), lambda i, j: (norm_idx, 0, 0)),
            pl.BlockSpec((None, D_MODEL, PROJ_TN), lambda i, j: (layer, 0, j)),
            pl.BlockSpec((PROJ_TM, RET_HEAD_QK), lambda i, j: (i % tiles_per_seq, 0)),
            pl.BlockSpec((PROJ_TM, RET_HEAD_QK), lambda i, j: (i % tiles_per_seq, 0)),
        ],
        out_specs=pl.BlockSpec((PROJ_TN // PROJ_GROUP, PROJ_TM, PROJ_GROUP), lambda i, j: (j, i, 0)),
        scratch_shapes=[pltpu.VMEM((PROJ_TM, D_MODEL), jnp.bfloat16),
                        pltpu.VMEM((PROJ_TM // ROW_CHUNK, ROW_CHUNK, D_MODEL), jnp.float32),
                        pltpu.SemaphoreType.DMA((PROJ_TM // ROW_CHUNK,))],
        compiler_params=_params("parallel", "arbitrary"),
        name="ret_projection",
    )(x2d, mod, norm_g, w_in, cos, sin)


def _scan_kernel(q_ref, k_ref, v_ref, g_ref, mask_ref, eps_ref, kdec_ref, cdec_ref, o_ref, state_ref):
    @pl.when(pl.program_id(2) == 0)
    def _():
        state_ref[...] = jnp.zeros_like(state_ref)

    mask = mask_ref[...]
    row_eps = eps_ref[...]
    k_dec = kdec_ref[...]
    chunk_dec = cdec_ref[...]
    state = state_ref[...]
    for c in range(q_ref.shape[0] // RET_CHUNK):
        rows = slice(c * RET_CHUNK, (c + 1) * RET_CHUNK)
        qc = q_ref[rows, :]
        kc = k_ref[rows, :]
        vc = jnp.concatenate([v_ref[grp, rows, :] for grp in range(v_ref.shape[0])], axis=1)
        gc = jnp.concatenate([g_ref[grp, rows, :] for grp in range(g_ref.shape[0])], axis=1)
        scores = lax.dot_general(qc, kc, (((1,), (1,)), ((), ())), preferred_element_type=jnp.float32)
        scores = (scores * mask).astype(jnp.bfloat16)
        y = (jnp.dot(scores, vc, preferred_element_type=jnp.float32)
             + jnp.dot(qc, state.astype(jnp.bfloat16), preferred_element_type=jnp.float32))
        k_decayed = (kc.astype(jnp.float32) * k_dec).astype(jnp.bfloat16)
        state = chunk_dec * state + lax.dot_general(
            k_decayed, vc, (((0,), (0,)), ((), ())), preferred_element_type=jnp.float32)
        mu = jnp.mean(y, axis=-1, keepdims=True)
        yc = y - mu
        var = jnp.mean(yc * yc, axis=-1, keepdims=True)
        yn = yc * lax.rsqrt(var + row_eps)
        z = (jax.nn.silu(gc.astype(jnp.float32)) * yn).astype(o_ref.dtype)
        for grp in range(o_ref.shape[0]):
            o_ref[grp, rows, :] = z[:, grp * PROJ_GROUP:(grp + 1) * PROJ_GROUP]
    state_ref[...] = state


def _ret_scan(proj, batch, seq, mask, row_eps, k_dec, chunk_dec):
    assert RET_HEAD_QK == PROJ_GROUP
    n_tok = proj.shape[1]
    steps = seq // RET_TS
    v_groups = RET_HEAD_V // PROJ_GROUP
    k_off = RET_QK_DIM // PROJ_GROUP
    v_off = 2 * RET_QK_DIM // RET_HEAD_V
    g_off = v_off + RET_HEADS
    return pl.pallas_call(
        _scan_kernel,
        out_shape=jax.ShapeDtypeStruct((RET_V_DIM // PROJ_GROUP, n_tok, PROJ_GROUP), jnp.bfloat16),
        grid=(batch, RET_HEADS, steps),
        in_specs=[
            pl.BlockSpec((None, RET_TS, PROJ_GROUP), lambda b, h, t: (h, b * steps + t, 0)),
            pl.BlockSpec((None, RET_TS, PROJ_GROUP), lambda b, h, t: (k_off + h, b * steps + t, 0)),
            pl.BlockSpec((v_groups, RET_TS, PROJ_GROUP), lambda b, h, t: (v_off + h, b * steps + t, 0)),
            pl.BlockSpec((v_groups, RET_TS, PROJ_GROUP), lambda b, h, t: (g_off + h, b * steps + t, 0)),
            pl.BlockSpec((None, RET_CHUNK, RET_CHUNK), lambda b, h, t: (h, 0, 0)),
            pl.BlockSpec((None, RET_CHUNK, 1), lambda b, h, t: (h, 0, 0)),
            pl.BlockSpec((None, RET_CHUNK, 1), lambda b, h, t: (h, 0, 0)),
            pl.BlockSpec((None, 1, 1), lambda b, h, t: (h, 0, 0)),
        ],
        out_specs=pl.BlockSpec((v_groups, RET_TS, PROJ_GROUP), lambda b, h, t: (h, b * steps + t, 0)),
        scratch_shapes=[pltpu.VMEM((RET_HEAD_QK, RET_HEAD_V), jnp.float32)],
        compiler_params=_params("parallel", "parallel", "arbitrary"),
        name="ret_scan",
    )(proj, proj, proj, proj, mask, row_eps, k_dec, chunk_dec)


def _out_kernel(z_ref, w_ref, x_ref, gate_ref, o_ref, wb_ref):
    @pl.when(pl.program_id(1) == 0)
    def _():
        wb_ref[...] = w_ref[...].astype(jnp.bfloat16)

    y = None
    for grp in range(z_ref.shape[0]):
        part = jnp.dot(z_ref[grp], wb_ref[grp * PROJ_GROUP:(grp + 1) * PROJ_GROUP, :],
                       preferred_element_type=jnp.float32)
        y = part if y is None else y + part
    o_ref[...] = x_ref[...] + gate_ref[0] * y


def _ret_output(z, x2d, mod, mod_idx, seq, w_out, layer):
    n_tok = x2d.shape[0]
    tiles_per_seq = seq // OUT_TM
    gate_off = 2 * D_MODEL // OUT_TN
    return pl.pallas_call(
        _out_kernel,
        out_shape=jax.ShapeDtypeStruct(x2d.shape, jnp.float32),
        grid=(D_MODEL // OUT_TN, n_tok // OUT_TM),
        in_specs=[
            pl.BlockSpec((RET_V_DIM // PROJ_GROUP, OUT_TM, PROJ_GROUP), lambda j, i: (0, i, 0)),
            pl.BlockSpec((None, RET_V_DIM, OUT_TN), lambda j, i: (layer, 0, j)),
            pl.BlockSpec((OUT_TM, OUT_TN), lambda j, i: (i, j)),
            pl.BlockSpec((1, 1, OUT_TN), lambda j, i: (mod_idx + i // tiles_per_seq, 0, gate_off + j)),
        ],
        out_specs=pl.BlockSpec((OUT_TM, OUT_TN), lambda j, i: (i, j)),
        scratch_shapes=[pltpu.VMEM((RET_V_DIM, OUT_TN), jnp.bfloat16)],
        compiler_params=_params("parallel", "arbitrary"),
        name="ret_output",
    )(z, w_out, x2d, mod)


def _rotary_tables(seq):
    inv = ROPE_BASE ** (-np.arange(0, RET_HEAD_QK, 2, dtype=np.float64) / RET_HEAD_QK)
    ang = np.arange(seq, dtype=np.float64)[:, None] * inv[None, :]
    cos = np.repeat(np.cos(ang), 2, axis=1)
    sin = np.stack([-np.sin(ang), np.sin(ang)], axis=-1).reshape(seq, RET_HEAD_QK)
    return cos.astype(np.float32), sin.astype(np.float32)


def _decay_tables():
    gamma = 1.0 - 2.0 ** (-5.0 - np.arange(RET_HEADS, dtype=np.float64))
    log_g = np.log(gamma)
    j = np.arange(RET_CHUNK, dtype=np.float64)
    causal = j[:, None] >= j[None, :]
    mask = np.where(causal[None], np.exp(-(j[None, None, :] + 1.0) * log_g[:, None, None]), 0.0)
    row_eps = (EPS * np.exp(-2.0 * (j[None, :] + 1.0) * log_g[:, None]))[:, :, None]
    k_dec = np.exp((RET_CHUNK - 1.0 - j[None, :]) * log_g[:, None])[:, :, None]
    chunk_dec = np.exp(RET_CHUNK * log_g)[:, None, None]
    return tuple(t.astype(np.float32) for t in (mask, row_eps, k_dec, chunk_dec))


@jax.jit
def kernel(x, c, ada_w, ada_b, norm_g, pool_w, pool_scale, ret_w_in, ret_w_out, mlp_w1, mlp_w2, final_g):
    batch, seq, d = x.shape
    assert d == D_MODEL
    assert seq % max(POOL_TM, MLP_TM, PROJ_TM, OUT_TM, RET_TS) == 0

    ada_w = ada_w.reshape(DEPTH * 2, d, 3 * d)
    ada_b = ada_b.reshape(DEPTH * 2, 1, 3 * d)
    c_lanes = jnp.broadcast_to(c[:, :, None], (batch, d, LANES))
    mod, cond = _ada_first_layer(c_lanes, ada_w, ada_b)

    cos, sin = _rotary_tables(seq)
    mask, row_eps, k_dec, chunk_dec = _decay_tables()
    norm_rows = norm_g.reshape(DEPTH * 2, 1, d)
    scale_rows = pool_scale.reshape(-1, 1, d)

    x2d = x.reshape(batch * seq, d)
    for i in range(DEPTH):
        mod = mod.reshape(2 * batch, 1, 3 * d)
        mix_idx, mlp_idx = 0, batch
        if i % 2 == 0:
            x2d = _pool_layer(x2d, mod, mix_idx, seq, norm_rows, 2 * i, pool_w, scale_rows, i // 2)
        else:
            proj = _ret_projection(x2d, mod, mix_idx, seq, norm_rows, 2 * i, ret_w_in, i // 2, cos, sin)
            z = _ret_scan(proj, batch, seq, mask, row_eps, k_dec, chunk_dec)
            x2d = _ret_output(z, x2d, mod, mix_idx, seq, ret_w_out, i // 2)
        if i < DEPTH - 1:
            x2d, mod = _mlp_layer(x2d, mod, mlp_idx, seq, norm_rows, 2 * i + 1, mlp_w1, mlp_w2, i,
                                  ada=(ada_w, ada_b, cond))
        else:
            x2d = _mlp_layer(x2d, mod, mlp_idx, seq, norm_rows, 2 * i + 1, mlp_w1, mlp_w2, i,
                             final_g=final_g.reshape(1, d))
    return x2d.reshape(batch, seq, d)
```

```python
import functools

import jax
import jax.numpy as jnp
import numpy as np
from jax import lax
from jax.experimental import pallas as pl
from jax.experimental.pallas import tpu as pltpu

D_MODEL = 2048
DEPTH = 4
POOL_WINDOWS = (2, 4, 8, 16)
POOL_GROUP = D_MODEL // len(POOL_WINDOWS)
POOL_HALO = 16
RET_HEADS = 8
RET_QK_DIM = D_MODEL
RET_V_DIM = 2 * D_MODEL
RET_HEAD_QK = RET_QK_DIM // RET_HEADS
RET_HEAD_V = RET_V_DIM // RET_HEADS
RET_PROJ = 2 * RET_QK_DIM + 2 * RET_V_DIM
ROPE_BASE = 10000.0
D_FF = 4 * D_MODEL
EPS = 1e-6
LANES = 128

V7X_VMEM_BYTES = 64 * 1024 * 1024
VMEM_LIMIT_BYTES = V7X_VMEM_BYTES - 4 * 1024 * 1024

ROW_CHUNK = 128
ADA_TK = 1024
ADA_FUSED_TK = 128
POOL_TM = 512
MLP_TM = 1024
MLP_TF = 512
MLP_TN = 512
PROJ_TM = 1024
PROJ_QK_TN = 512
PROJ_VG_TN = 1024
PROJ_GROUP = 256
RET_TS = 1024
RET_CHUNK = 256
OUT_TM = 1024
OUT_TN = 512


def _params(*semantics):
    return pltpu.CompilerParams(dimension_semantics=semantics, vmem_limit_bytes=VMEM_LIMIT_BYTES)


def _norm_modulate(x, g, mod_row):
    shift = mod_row[:, 0:D_MODEL]
    scale = mod_row[:, D_MODEL:2 * D_MODEL]
    inv = lax.rsqrt(jnp.mean(x * x, axis=-1, keepdims=True) + EPS)
    return x * inv * (g * (1.0 + scale)) + shift


def _ada_columns(w, cond_ref, bias):
    rows = []
    for b in range(cond_ref.shape[0]):
        cond = cond_ref[b]
        rows.append(jnp.concatenate(
            [jnp.sum(w[:, lo:lo + LANES] * cond, axis=0, keepdims=True) for lo in range(0, w.shape[1], LANES)],
            axis=1))
    return jnp.concatenate(rows, axis=0) + bias


def _ada_kernel(c_ref, w_ref, b_ref, o_ref, cond_ref):
    cond_ref[...] = jax.nn.silu(c_ref[...])
    o_ref[...] = _ada_columns(w_ref[...], cond_ref, b_ref[...])


def _ada_first_layer(c_lanes, ada_w, ada_b):
    batch = c_lanes.shape[0]
    n_out = ada_w.shape[2]
    return pl.pallas_call(
        _ada_kernel,
        out_shape=(jax.ShapeDtypeStruct((2, batch, n_out), jnp.float32),
                   jax.ShapeDtypeStruct(c_lanes.shape, jnp.float32)),
        grid=(2, n_out // ADA_TK),
        in_specs=[
            pl.BlockSpec(c_lanes.shape, lambda l, j: (0, 0, 0)),
            pl.BlockSpec((None, D_MODEL, ADA_TK), lambda l, j: (l, 0, j)),
            pl.BlockSpec((None, 1, ADA_TK), lambda l, j: (l, 0, j)),
        ],
        out_specs=(pl.BlockSpec((None, batch, ADA_TK), lambda l, j: (l, 0, j)),
                   pl.BlockSpec(c_lanes.shape, lambda l, j: (0, 0, 0))),
        compiler_params=_params("arbitrary", "arbitrary"),
        name="ada_modulation",
    )(c_lanes, ada_w, ada_b)


def _pool_kernel(tiles_per_seq, x_ref, halo_ref, mod_ref, g_ref, w_ref, ls_ref, o_ref, wb_ref):
    i = pl.program_id(0)

    @pl.when(i == 0)
    def _():
        wb_ref[...] = w_ref[...].astype(jnp.bfloat16)

    tile_in_seq = i % tiles_per_seq
    tm = x_ref.shape[0]
    mod_row = mod_ref[0]
    g = g_ref[0]
    x = x_ref[...]
    h = _norm_modulate(x, g, mod_row)
    h_halo = _norm_modulate(halo_ref[...], g, mod_row)
    h_halo = jnp.where(tile_in_seq == 0, 0.0, h_halo)
    out_scale = mod_row[:, 2 * D_MODEL:3 * D_MODEL] * ls_ref[0]
    pos = tile_in_seq * tm + lax.broadcasted_iota(jnp.int32, (tm, 1), 0)
    for gi, window in enumerate(POOL_WINDOWS):
        cols = slice(gi * POOL_GROUP, (gi + 1) * POOL_GROUP)
        hg = h[:, cols]
        acc = jnp.concatenate([h_halo[:, cols], hg], axis=0)
        shift = 1
        while shift < window:
            acc = acc + pltpu.roll(acc, shift, axis=0)
            shift *= 2
        cnt = jnp.minimum(pos + 1, window).astype(jnp.float32)
        p = acc[POOL_HALO:, :] / cnt - hg
        y = jnp.dot(p.astype(jnp.bfloat16), wb_ref[gi], preferred_element_type=jnp.float32)
        o_ref[:, cols] = x[:, cols] + out_scale[:, cols] * y


def _pool_layer(x2d, mod, mod_idx, seq, norm_g, norm_idx, pool_w, pool_scale, pool_idx):
    n_tok = x2d.shape[0]
    tiles_per_seq = seq // POOL_TM
    halo_blocks_per_tile = POOL_TM // POOL_HALO
    n_grp = len(POOL_WINDOWS)
    return pl.pallas_call(
        functools.partial(_pool_kernel, tiles_per_seq),
        out_shape=jax.ShapeDtypeStruct(x2d.shape, jnp.float32),
        grid=(n_tok // POOL_TM,),
        in_specs=[
            pl.BlockSpec((POOL_TM, D_MODEL), lambda i: (i, 0)),
            pl.BlockSpec((POOL_HALO, D_MODEL), lambda i: (jnp.maximum(i * halo_blocks_per_tile - 1, 0), 0)),
            pl.BlockSpec((1, 1, 3 * D_MODEL), lambda i: (mod_idx + i // tiles_per_seq, 0, 0)),
            pl.BlockSpec((1, 1, D_MODEL), lambda i: (norm_idx, 0, 0)),
            pl.BlockSpec((None, n_grp, POOL_GROUP, POOL_GROUP), lambda i: (pool_idx, 0, 0, 0)),
            pl.BlockSpec((1, 1, D_MODEL), lambda i: (pool_idx, 0, 0)),
        ],
        out_specs=pl.BlockSpec((POOL_TM, D_MODEL), lambda i: (i, 0)),
        scratch_shapes=[pltpu.VMEM((n_grp, POOL_GROUP, POOL_GROUP), jnp.bfloat16)],
        compiler_params=_params("arbitrary"),
        name="pool_mixer",
    )(x2d, x2d, mod, norm_g, pool_w, pool_scale)


def _row_chunk_copies(x_hbm, first_row, dst_of_chunk, sems):
    return [
        pltpu.make_async_copy(x_hbm.at[pl.ds(first_row + k * ROW_CHUNK, ROW_CHUNK), :], dst_of_chunk(k), sems.at[k])
        for k in range(sems.shape[0])
    ]


def _mlp_kernel(last, x_hbm, mod_ref, g_ref, w1_ref, w2_ref, *refs):
    if last:
        fg_ref, o_ref, h_ref, sems = refs
    else:
        wa_ref, ba_ref, cond_ref, o_ref, next_mod_ref, h_ref, sems = refs
    j = pl.program_id(1)

    @pl.when(j == 0)
    def _():
        first_row = pl.multiple_of(pl.program_id(0) * MLP_TM, MLP_TM)
        copies = _row_chunk_copies(x_hbm, first_row, lambda k: o_ref.at[pl.ds(k * ROW_CHUNK, ROW_CHUNK), :], sems)
        for cp in copies:
            cp.start()
        for k, cp in enumerate(copies):
            cp.wait()
            rows = pl.ds(k * ROW_CHUNK, ROW_CHUNK)
            h_ref[rows, :] = _norm_modulate(o_ref[rows, :], g_ref[0], mod_ref[0]).astype(jnp.bfloat16)

    if not last:
        next_mod_ref[...] = _ada_columns(wa_ref[...], cond_ref, ba_ref[...])

    a = jnp.dot(h_ref[...], w1_ref[...].astype(jnp.bfloat16), preferred_element_type=jnp.float32)
    a = jnp.maximum(a, 0.0)
    a = (a * a).astype(jnp.bfloat16)
    w2 = w2_ref[...].astype(jnp.bfloat16)
    for n in range(0, D_MODEL, MLP_TN):
        gate = mod_ref[0, :, 2 * D_MODEL + n:2 * D_MODEL + n + MLP_TN]
        o_ref[:, n:n + MLP_TN] += gate * jnp.dot(a, w2[:, n:n + MLP_TN], preferred_element_type=jnp.float32)

    if last:
        @pl.when(j == pl.num_programs(1) - 1)
        def _():
            y = o_ref[...]
            inv = lax.rsqrt(jnp.mean(y * y, axis=-1, keepdims=True) + EPS)
            o_ref[...] = y * inv * fg_ref[...]


def _mlp_layer(x2d, mod, mod_idx, seq, norm_g, norm_idx, w1, w2, layer, final_g=None, ada=None):
    n_tok = x2d.shape[0]
    tiles_per_seq = seq // MLP_TM
    steps = D_FF // MLP_TF
    last = ada is None
    in_specs = [
        pl.BlockSpec(memory_space=pl.ANY),
        pl.BlockSpec((1, 1, 3 * D_MODEL), lambda i, j: (mod_idx + i // tiles_per_seq, 0, 0)),
        pl.BlockSpec((1, 1, D_MODEL), lambda i, j: (norm_idx, 0, 0)),
        pl.BlockSpec((None, D_MODEL, MLP_TF), lambda i, j: (layer, 0, j)),
        pl.BlockSpec((None, MLP_TF, D_MODEL), lambda i, j: (layer, j, 0)),
    ]
    x_spec = pl.BlockSpec((MLP_TM, D_MODEL), lambda i, j: (i, 0))
    x_shape = jax.ShapeDtypeStruct(x2d.shape, jnp.float32)
    if last:
        operands = (x2d, mod, norm_g, w1, w2, final_g)
        in_specs.append(pl.BlockSpec((1, D_MODEL), lambda i, j: (0, 0)))
        out_specs, out_shape = x_spec, x_shape
    else:
        ada_w, ada_b, cond = operands_ada = ada
        batch = cond.shape[0]
        blocks_per_row = 3 * D_MODEL // ADA_FUSED_TK
        n_blocks = 2 * blocks_per_row
        assert n_blocks <= (n_tok // MLP_TM) * steps

        def ada_block(i, j):
            blk = jnp.minimum(i * steps + j, n_blocks - 1)
            return 2 * (layer + 1) + blk // blocks_per_row, 0, blk % blocks_per_row

        def next_mod_block(i, j):
            row, _, col = ada_block(i, j)
            return row - 2 * (layer + 1), 0, col

        operands = (x2d, mod, norm_g, w1, w2) + operands_ada
        in_specs += [
            pl.BlockSpec((None, D_MODEL, ADA_FUSED_TK), ada_block),
            pl.BlockSpec((None, 1, ADA_FUSED_TK), ada_block),
            pl.BlockSpec(cond.shape, lambda i, j: (0, 0, 0)),
        ]
        out_specs = (x_spec, pl.BlockSpec((None, batch, ADA_FUSED_TK), next_mod_block))
        out_shape = (x_shape, jax.ShapeDtypeStruct((2, batch, 3 * D_MODEL), jnp.float32))
    return pl.pallas_call(
        functools.partial(_mlp_kernel, last),
        out_shape=out_shape,
        grid=(n_tok // MLP_TM, steps),
        in_specs=in_specs,
        out_specs=out_specs,
        scratch_shapes=[pltpu.VMEM((MLP_TM, D_MODEL), jnp.bfloat16),
                        pltpu.SemaphoreType.DMA((MLP_TM // ROW_CHUNK,))],
        compiler_params=_params("arbitrary", "arbitrary"),
        name="mlp_final" if last else "mlp",
    )(*operands)


def _proj_kernel(x_hbm, mod_ref, g_ref, wqk_ref, wvg_ref, cos_ref, sin_ref, oqk_ref, ovg_ref, h_ref, xbuf_ref, sems):
    j = pl.program_id(1)
    n_q = RET_QK_DIM // PROJ_QK_TN

    @pl.when(j == 0)
    def _():
        first_row = pl.multiple_of(pl.program_id(0) * PROJ_TM, PROJ_TM)
        copies = _row_chunk_copies(x_hbm, first_row, lambda k: xbuf_ref.at[k], sems)
        for cp in copies:
            cp.start()
        for k, cp in enumerate(copies):
            cp.wait()
            rows = pl.ds(k * ROW_CHUNK, ROW_CHUNK)
            h_ref[rows, :] = _norm_modulate(xbuf_ref[k], g_ref[0], mod_ref[0]).astype(jnp.bfloat16)

    w = jnp.concatenate([wqk_ref[...].astype(jnp.bfloat16), wvg_ref[...].astype(jnp.bfloat16)], axis=1)
    r = jnp.dot(h_ref[...], w, preferred_element_type=jnp.float32)

    k_scale = jnp.where(j >= n_q, RET_HEAD_QK ** -0.5, 1.0)
    cos = cos_ref[...] * k_scale
    sin = sin_ref[...] * k_scale
    swap = lax.broadcasted_iota(jnp.int32, (PROJ_TM, LANES), 1) ^ 1
    for lo in range(0, PROJ_QK_TN, LANES):
        t = lo % RET_HEAD_QK
        xs = r[:, lo:lo + LANES]
        partner = jnp.take_along_axis(xs, swap, axis=1)
        oqk_ref[lo // PROJ_GROUP, :, t:t + LANES] = (
            xs * cos[:, t:t + LANES] + partner * sin[:, t:t + LANES]).astype(oqk_ref.dtype)
    for grp in range(PROJ_VG_TN // PROJ_GROUP):
        lo = PROJ_QK_TN + grp * PROJ_GROUP
        ovg_ref[grp] = r[:, lo:lo + PROJ_GROUP].astype(ovg_ref.dtype)


def _ret_projection(x2d, mod, mod_idx, seq, norm_g, norm_idx, w_in, layer, cos, sin):
    n_tok = x2d.shape[0]
    tiles_per_seq = seq // PROJ_TM
    steps = 2 * RET_QK_DIM // PROJ_QK_TN
    assert steps == 2 * RET_V_DIM // PROJ_VG_TN
    vg_off = 2 * RET_QK_DIM // PROJ_VG_TN
    return pl.pallas_call(
        _proj_kernel,
        out_shape=(jax.ShapeDtypeStruct((2 * RET_QK_DIM // PROJ_GROUP, n_tok, PROJ_GROUP), jnp.bfloat16),
                   jax.ShapeDtypeStruct((2 * RET_V_DIM // PROJ_GROUP, n_tok, PROJ_GROUP), jnp.bfloat16)),
        grid=(n_tok // PROJ_TM, steps),
        in_specs=[
            pl.BlockSpec(memory_space=pl.ANY),
            pl.BlockSpec((1, 1, 3 * D_MODEL), lambda i, j: (mod_idx + i // tiles_per_seq, 0, 0)),
            pl.BlockSpec((1, 1, D_MODEL), lambda i, j: (norm_idx, 0, 0)),
            pl.BlockSpec((None, D_MODEL, PROJ_QK_TN), lambda i, j: (layer, 0, j)),
            pl.BlockSpec((None, D_MODEL, PROJ_VG_TN), lambda i, j: (layer, 0, vg_off + j)),
            pl.BlockSpec((PROJ_TM, RET_HEAD_QK), lambda i, j: (i % tiles_per_seq, 0)),
            pl.BlockSpec((PROJ_TM, RET_HEAD_QK), lambda i, j: (i % tiles_per_seq, 0)),
        ],
        out_specs=(pl.BlockSpec((PROJ_QK_TN // PROJ_GROUP, PROJ_TM, PROJ_GROUP), lambda i, j: (j, i, 0)),
                   pl.BlockSpec((PROJ_VG_TN // PROJ_GROUP, PROJ_TM, PROJ_GROUP), lambda i, j: (j, i, 0))),
        scratch_shapes=[pltpu.VMEM((PROJ_TM, D_MODEL), jnp.bfloat16),
                        pltpu.VMEM((PROJ_TM // ROW_CHUNK, ROW_CHUNK, D_MODEL), jnp.float32),
                        pltpu.SemaphoreType.DMA((PROJ_TM // ROW_CHUNK,))],
        compiler_params=_params("parallel", "arbitrary"),
        name="ret_projection",
    )(x2d, mod, norm_g, w_in, w_in, cos, sin)


def _scan_kernel(q_ref, k_ref, v_ref, g_ref, mask_ref, eps_ref, kdec_ref, cdec_ref, o_ref, state_ref):
    @pl.when(pl.program_id(2) == 0)
    def _():
        state_ref[...] = jnp.zeros_like(state_ref)

    mask = mask_ref[...]
    row_eps = eps_ref[...]
    k_dec = kdec_ref[...]
    chunk_dec = cdec_ref[...]
    state = state_ref[...]
    for c in range(q_ref.shape[0] // RET_CHUNK):
        rows = slice(c * RET_CHUNK, (c + 1) * RET_CHUNK)
        qc = q_ref[rows, :]
        kc = k_ref[rows, :]
        vc = jnp.concatenate([v_ref[grp, rows, :] for grp in range(v_ref.shape[0])], axis=1)
        gc = jnp.concatenate([g_ref[grp, rows, :] for grp in range(g_ref.shape[0])], axis=1)
        scores = lax.dot_general(qc, kc, (((1,), (1,)), ((), ())), preferred_element_type=jnp.float32)
        scores = (scores * mask).astype(jnp.bfloat16)
        y = (jnp.dot(scores, vc, preferred_element_type=jnp.float32)
             + jnp.dot(qc, state.astype(jnp.bfloat16), preferred_element_type=jnp.float32))
        k_decayed = (kc.astype(jnp.float32) * k_dec).astype(jnp.bfloat16)
        state = chunk_dec * state + lax.dot_general(
            k_decayed, vc, (((0,), (0,)), ((), ())), preferred_element_type=jnp.float32)
        mu = jnp.mean(y, axis=-1, keepdims=True)
        yc = y - mu
        var = jnp.mean(yc * yc, axis=-1, keepdims=True)
        yn = yc * lax.rsqrt(var + row_eps)
        z = (jax.nn.silu(gc.astype(jnp.float32)) * yn).astype(o_ref.dtype)
        for grp in range(o_ref.shape[0]):
            o_ref[grp, rows, :] = z[:, grp * PROJ_GROUP:(grp + 1) * PROJ_GROUP]
    state_ref[...] = state


def _ret_scan(proj_qk, proj_vg, batch, seq, mask, row_eps, k_dec, chunk_dec):
    assert RET_HEAD_QK == PROJ_GROUP
    n_tok = proj_qk.shape[1]
    steps = seq // RET_TS
    v_groups = RET_HEAD_V // PROJ_GROUP
    k_off = RET_HEADS
    v_off = 0
    g_off = RET_HEADS
    return pl.pallas_call(
        _scan_kernel,
        out_shape=jax.ShapeDtypeStruct((RET_V_DIM // PROJ_GROUP, n_tok, PROJ_GROUP), jnp.bfloat16),
        grid=(batch, RET_HEADS, steps),
        in_specs=[
            pl.BlockSpec((None, RET_TS, PROJ_GROUP), lambda b, h, t: (h, b * steps + t, 0)),
            pl.BlockSpec((None, RET_TS, PROJ_GROUP), lambda b, h, t: (k_off + h, b * steps + t, 0)),
            pl.BlockSpec((v_groups, RET_TS, PROJ_GROUP), lambda b, h, t: (v_off + h, b * steps + t, 0)),
            pl.BlockSpec((v_groups, RET_TS, PROJ_GROUP), lambda b, h, t: (g_off + h, b * steps + t, 0)),
            pl.BlockSpec((None, RET_CHUNK, RET_CHUNK), lambda b, h, t: (h, 0, 0)),
            pl.BlockSpec((None, RET_CHUNK, 1), lambda b, h, t: (h, 0, 0)),
            pl.BlockSpec((None, RET_CHUNK, 1), lambda b, h, t: (h, 0, 0)),
            pl.BlockSpec((None, 1, 1), lambda b, h, t: (h, 0, 0)),
        ],
        out_specs=pl.BlockSpec((v_groups, RET_TS, PROJ_GROUP), lambda b, h, t: (h, b * steps + t, 0)),
        scratch_shapes=[pltpu.VMEM((RET_HEAD_QK, RET_HEAD_V), jnp.float32)],
        compiler_params=_params("parallel", "parallel", "arbitrary"),
        name="ret_scan",
    )(proj_qk, proj_qk, proj_vg, proj_vg, mask, row_eps, k_dec, chunk_dec)


def _out_kernel(z_ref, w_ref, x_ref, gate_ref, o_ref, wb_ref):
    @pl.when(pl.program_id(1) == 0)
    def _():
        wb_ref[...] = w_ref[...].astype(jnp.bfloat16)

    y = None
    for grp in range(z_ref.shape[0]):
        part = jnp.dot(z_ref[grp], wb_ref[grp * PROJ_GROUP:(grp + 1) * PROJ_GROUP, :],
                       preferred_element_type=jnp.float32)
        y = part if y is None else y + part
    o_ref[...] = x_ref[...] + gate_ref[0] * y


def _ret_output(z, x2d, mod, mod_idx, seq, w_out, layer):
    n_tok = x2d.shape[0]
    tiles_per_seq = seq // OUT_TM
    gate_off = 2 * D_MODEL // OUT_TN
    return pl.pallas_call(
        _out_kernel,
        out_shape=jax.ShapeDtypeStruct(x2d.shape, jnp.float32),
        grid=(D_MODEL // OUT_TN, n_tok // OUT_TM),
        in_specs=[
            pl.BlockSpec((RET_V_DIM // PROJ_GROUP, OUT_TM, PROJ_GROUP), lambda j, i: (0, i, 0)),
            pl.BlockSpec((None, RET_V_DIM, OUT_TN), lambda j, i: (layer, 0, j)),
            pl.BlockSpec((OUT_TM, OUT_TN), lambda j, i: (i, j)),
            pl.BlockSpec((1, 1, OUT_TN), lambda j, i: (mod_idx + i // tiles_per_seq, 0, gate_off + j)),
        ],
        out_specs=pl.BlockSpec((OUT_TM, OUT_TN), lambda j, i: (i, j)),
        scratch_shapes=[pltpu.VMEM((RET_V_DIM, OUT_TN), jnp.bfloat16)],
        compiler_params=_params("parallel", "arbitrary"),
        name="ret_output",
    )(z, w_out, x2d, mod)


def _rotary_tables(seq):
    inv = ROPE_BASE ** (-np.arange(0, RET_HEAD_QK, 2, dtype=np.float64) / RET_HEAD_QK)
    ang = np.arange(seq, dtype=np.float64)[:, None] * inv[None, :]
    cos = np.repeat(np.cos(ang), 2, axis=1)
    sin = np.stack([-np.sin(ang), np.sin(ang)], axis=-1).reshape(seq, RET_HEAD_QK)
    return cos.astype(np.float32), sin.astype(np.float32)


def _decay_tables():
    gamma = 1.0 - 2.0 ** (-5.0 - np.arange(RET_HEADS, dtype=np.float64))
    log_g = np.log(gamma)
    j = np.arange(RET_CHUNK, dtype=np.float64)
    causal = j[:, None] >= j[None, :]
    mask = np.where(causal[None], np.exp(-(j[None, None, :] + 1.0) * log_g[:, None, None]), 0.0)
    row_eps = (EPS * np.exp(-2.0 * (j[None, :] + 1.0) * log_g[:, None]))[:, :, None]
    k_dec = np.exp((RET_CHUNK - 1.0 - j[None, :]) * log_g[:, None])[:, :, None]
    chunk_dec = np.exp(RET_CHUNK * log_g)[:, None, None]
    return tuple(t.astype(np.float32) for t in (mask, row_eps, k_dec, chunk_dec))


@jax.jit
def kernel(x, c, ada_w, ada_b, norm_g, pool_w, pool_scale, ret_w_in, ret_w_out, mlp_w1, mlp_w2, final_g):
    batch, seq, d = x.shape
    assert d == D_MODEL
    assert seq % max(POOL_TM, MLP_TM, PROJ_TM, OUT_TM, RET_TS) == 0

    ada_w = ada_w.reshape(DEPTH * 2, d, 3 * d)
    ada_b = ada_b.reshape(DEPTH * 2, 1, 3 * d)
    c_lanes = jnp.broadcast_to(c[:, :, None], (batch, d, LANES))
    mod, cond = _ada_first_layer(c_lanes, ada_w, ada_b)

    cos, sin = _rotary_tables(seq)
    mask, row_eps, k_dec, chunk_dec = _decay_tables()
    norm_rows = norm_g.reshape(DEPTH * 2, 1, d)
    scale_rows = pool_scale.reshape(-1, 1, d)

    x2d = x.reshape(batch * seq, d)
    for i in range(DEPTH):
        mod = mod.reshape(2 * batch, 1, 3 * d)
        mix_idx, mlp_idx = 0, batch
        if i % 2 == 0:
            x2d = _pool_layer(x2d, mod, mix_idx, seq, norm_rows, 2 * i, pool_w, scale_rows, i // 2)
        else:
            proj_qk, proj_vg = _ret_projection(x2d, mod, mix_idx, seq, norm_rows, 2 * i, ret_w_in, i // 2, cos, sin)
            z = _ret_scan(proj_qk, proj_vg, batch, seq, mask, row_eps, k_dec, chunk_dec)
            x2d = _ret_output(z, x2d, mod, mix_idx, seq, ret_w_out, i // 2)
        if i < DEPTH - 1:
            x2d, mod = _mlp_layer(x2d, mod, mlp_idx, seq, norm_rows, 2 * i + 1, mlp_w1, mlp_w2, i,
                                  ada=(ada_w, ada_b, cond))
        else:
            x2d = _mlp_layer(x2d, mod, mlp_idx, seq, norm_rows, 2 * i + 1, mlp_w1, mlp_w2, i,
                             final_g=final_g.reshape(1, d))
    return x2d.reshape(batch, seq, d)
```

```python
import functools

import jax
import jax.numpy as jnp
import numpy as np
from jax import lax
from jax.experimental import pallas as pl
from jax.experimental.pallas import tpu as pltpu

D_MODEL = 2048
DEPTH = 4
POOL_WINDOWS = (2, 4, 8, 16)
POOL_GROUP = D_MODEL // len(POOL_WINDOWS)
POOL_HALO = 16
RET_HEADS = 8
RET_QK_DIM = D_MODEL
RET_V_DIM = 2 * D_MODEL
RET_HEAD_QK = RET_QK_DIM // RET_HEADS
RET_HEAD_V = RET_V_DIM // RET_HEADS
RET_PROJ = 2 * RET_QK_DIM + 2 * RET_V_DIM
ROPE_BASE = 10000.0
D_FF = 4 * D_MODEL
EPS = 1e-6
LANES = 128

V7X_VMEM_BYTES = 64 * 1024 * 1024
VMEM_LIMIT_BYTES = V7X_VMEM_BYTES - 4 * 1024 * 1024

ROW_CHUNK = 128
NORM_COLS = 256
ADA_TK = 1024
ADA_FUSED_TK = 128
POOL_TM = 512
MLP_TM = 1024
MLP_TF = 512
MLP_TN = 512
PROJ_TM = 1024
PROJ_QK_TN = 512
PROJ_VG_TN = 1024
PROJ_GROUP = 256
RET_TS = 1024
RET_CHUNK = 256
OUT_TM = 1024
OUT_TN = 512


def _params(*semantics):
    return pltpu.CompilerParams(dimension_semantics=semantics, vmem_limit_bytes=VMEM_LIMIT_BYTES)


def _norm_modulate(x, g, mod_row):
    shift = mod_row[:, 0:D_MODEL]
    scale = mod_row[:, D_MODEL:2 * D_MODEL]
    inv = lax.rsqrt(jnp.mean(x * x, axis=-1, keepdims=True) + EPS)
    return x * inv * (g * (1.0 + scale)) + shift


def _ada_columns(w, cond_ref, bias):
    rows = []
    for b in range(cond_ref.shape[0]):
        cond = cond_ref[b]
        rows.append(jnp.concatenate(
            [jnp.sum(w[:, lo:lo + LANES] * cond, axis=0, keepdims=True) for lo in range(0, w.shape[1], LANES)],
            axis=1))
    return jnp.concatenate(rows, axis=0) + bias


def _ada_kernel(c_ref, w_ref, b_ref, o_ref, cond_ref):
    cond_ref[...] = jax.nn.silu(c_ref[...])
    o_ref[...] = _ada_columns(w_ref[...], cond_ref, b_ref[...])


def _ada_first_layer(c_lanes, ada_w, ada_b):
    batch = c_lanes.shape[0]
    n_out = ada_w.shape[2]
    return pl.pallas_call(
        _ada_kernel,
        out_shape=(jax.ShapeDtypeStruct((2, batch, n_out), jnp.float32),
                   jax.ShapeDtypeStruct(c_lanes.shape, jnp.float32)),
        grid=(2, n_out // ADA_TK),
        in_specs=[
            pl.BlockSpec(c_lanes.shape, lambda l, j: (0, 0, 0)),
            pl.BlockSpec((None, D_MODEL, ADA_TK), lambda l, j: (l, 0, j)),
            pl.BlockSpec((None, 1, ADA_TK), lambda l, j: (l, 0, j)),
        ],
        out_specs=(pl.BlockSpec((None, batch, ADA_TK), lambda l, j: (l, 0, j)),
                   pl.BlockSpec(c_lanes.shape, lambda l, j: (0, 0, 0))),
        compiler_params=_params("arbitrary", "arbitrary"),
        name="ada_modulation",
    )(c_lanes, ada_w, ada_b)


def _pool_kernel(tiles_per_seq, x_ref, halo_ref, mod_ref, g_ref, w_ref, ls_ref, o_ref, wb_ref):
    i = pl.program_id(0)

    @pl.when(i == 0)
    def _():
        wb_ref[...] = w_ref[...].astype(jnp.bfloat16)

    tile_in_seq = i % tiles_per_seq
    tm = x_ref.shape[0]
    mod_row = mod_ref[0]
    g = g_ref[0]
    x = x_ref[...]
    h = _norm_modulate(x, g, mod_row)
    h_halo = _norm_modulate(halo_ref[...], g, mod_row)
    h_halo = jnp.where(tile_in_seq == 0, 0.0, h_halo)
    out_scale = mod_row[:, 2 * D_MODEL:3 * D_MODEL] * ls_ref[0]
    pos = tile_in_seq * tm + lax.broadcasted_iota(jnp.int32, (tm, 1), 0)
    for gi, window in enumerate(POOL_WINDOWS):
        cols = slice(gi * POOL_GROUP, (gi + 1) * POOL_GROUP)
        hg = h[:, cols]
        acc = jnp.concatenate([h_halo[:, cols], hg], axis=0)
        shift = 1
        while shift < window:
            acc = acc + pltpu.roll(acc, shift, axis=0)
            shift *= 2
        cnt = jnp.minimum(pos + 1, window).astype(jnp.float32)
        p = acc[POOL_HALO:, :] / cnt - hg
        y = jnp.dot(p.astype(jnp.bfloat16), wb_ref[gi], preferred_element_type=jnp.float32)
        o_ref[:, cols] = x[:, cols] + out_scale[:, cols] * y


def _pool_layer(x2d, mod, mod_idx, seq, norm_g, norm_idx, pool_w, pool_scale, pool_idx):
    n_tok = x2d.shape[0]
    tiles_per_seq = seq // POOL_TM
    halo_blocks_per_tile = POOL_TM // POOL_HALO
    n_grp = len(POOL_WINDOWS)
    return pl.pallas_call(
        functools.partial(_pool_kernel, tiles_per_seq),
        out_shape=jax.ShapeDtypeStruct(x2d.shape, jnp.float32),
        grid=(n_tok // POOL_TM,),
        in_specs=[
            pl.BlockSpec((POOL_TM, D_MODEL), lambda i: (i, 0)),
            pl.BlockSpec((POOL_HALO, D_MODEL), lambda i: (jnp.maximum(i * halo_blocks_per_tile - 1, 0), 0)),
            pl.BlockSpec((1, 1, 3 * D_MODEL), lambda i: (mod_idx + i // tiles_per_seq, 0, 0)),
            pl.BlockSpec((1, 1, D_MODEL), lambda i: (norm_idx, 0, 0)),
            pl.BlockSpec((None, n_grp, POOL_GROUP, POOL_GROUP), lambda i: (pool_idx, 0, 0, 0)),
            pl.BlockSpec((1, 1, D_MODEL), lambda i: (pool_idx, 0, 0)),
        ],
        out_specs=pl.BlockSpec((POOL_TM, D_MODEL), lambda i: (i, 0)),
        scratch_shapes=[pltpu.VMEM((n_grp, POOL_GROUP, POOL_GROUP), jnp.bfloat16)],
        compiler_params=_params("arbitrary"),
        name="pool_mixer",
    )(x2d, x2d, mod, norm_g, pool_w, pool_scale)


def _row_chunk_copies(x_hbm, first_row, dst_of_chunk, sems):
    return [
        pltpu.make_async_copy(x_hbm.at[pl.ds(first_row + k * ROW_CHUNK, ROW_CHUNK), :], dst_of_chunk(k), sems.at[k])
        for k in range(sems.shape[0])
    ]


def _mlp_kernel(last, x_hbm, mod_ref, g_ref, w1_ref, w2_ref, *refs):
    if last:
        fg_ref, o_ref, h_ref, x_ref, sems = refs
    else:
        wa_ref, ba_ref, cond_ref, o_ref, next_mod_ref, h_ref, x_ref, sems = refs
    i = pl.program_id(0)
    j = pl.program_id(1)

    def tile_copies(tile):
        slot = tile % 2
        first_row = pl.multiple_of(tile * MLP_TM, MLP_TM)
        return _row_chunk_copies(x_hbm, first_row, lambda k: x_ref.at[slot, pl.ds(k * ROW_CHUNK, ROW_CHUNK), :],
                                 sems.at[slot])

    def finish(pre, base_ref):
        a = jnp.maximum(pre, 0.0)
        a = (a * a).astype(jnp.bfloat16)
        w2 = w2_ref[...].astype(jnp.bfloat16)
        for n in range(0, D_MODEL, MLP_TN):
            cols = slice(n, n + MLP_TN)
            gate = mod_ref[0, :, 2 * D_MODEL + n:2 * D_MODEL + n + MLP_TN]
            o_ref[:, cols] = base_ref[:, cols] + gate * jnp.dot(a, w2[:, cols], preferred_element_type=jnp.float32)
        if not last:
            next_mod_ref[...] = _ada_columns(wa_ref[...], cond_ref, ba_ref[...])

    @pl.when(j == 0)
    def _():
        @pl.when(i == 0)
        def _():
            for cp in tile_copies(i):
                cp.start()

        @pl.when(i + 1 < pl.num_programs(0))
        def _():
            for cp in tile_copies(i + 1):
                cp.start()

        for cp in tile_copies(i):
            cp.wait()
        x_tile = x_ref.at[i % 2]
        x = x_tile[...]
        inv = lax.rsqrt(jnp.mean(x * x, axis=-1, keepdims=True) + EPS)
        mod_row = mod_ref[0]
        col_gain = g_ref[0] * (1.0 + mod_row[:, D_MODEL:2 * D_MODEL])
        acc = None
        for lo in range(0, D_MODEL, NORM_COLS):
            cols = slice(lo, lo + NORM_COLS)
            hk = (x_tile[:, cols] * inv * col_gain[:, cols] + mod_row[:, cols]).astype(jnp.bfloat16)
            h_ref[:, cols] = hk
            part = jnp.dot(hk, w1_ref[cols, :].astype(jnp.bfloat16), preferred_element_type=jnp.float32)
            acc = part if acc is None else acc + part
        finish(acc, x_tile)

    @pl.when(j > 0)
    def _():
        finish(jnp.dot(h_ref[...], w1_ref[...].astype(jnp.bfloat16), preferred_element_type=jnp.float32), o_ref)

    if last:
        @pl.when(j == pl.num_programs(1) - 1)
        def _():
            y = o_ref[...]
            inv = lax.rsqrt(jnp.mean(y * y, axis=-1, keepdims=True) + EPS)
            o_ref[...] = y * inv * fg_ref[...]


def _mlp_layer(x2d, mod, mod_idx, seq, norm_g, norm_idx, w1, w2, layer, final_g=None, ada=None):
    n_tok = x2d.shape[0]
    tiles_per_seq = seq // MLP_TM
    steps = D_FF // MLP_TF
    last = ada is None
    in_specs = [
        pl.BlockSpec(memory_space=pl.ANY),
        pl.BlockSpec((1, 1, 3 * D_MODEL), lambda i, j: (mod_idx + i // tiles_per_seq, 0, 0)),
        pl.BlockSpec((1, 1, D_MODEL), lambda i, j: (norm_idx, 0, 0)),
        pl.BlockSpec((None, D_MODEL, MLP_TF), lambda i, j: (layer, 0, j)),
        pl.BlockSpec((None, MLP_TF, D_MODEL), lambda i, j: (layer, j, 0)),
    ]
    x_spec = pl.BlockSpec((MLP_TM, D_MODEL), lambda i, j: (i, 0))
    x_shape = jax.ShapeDtypeStruct(x2d.shape, jnp.float32)
    if last:
        operands = (x2d, mod, norm_g, w1, w2, final_g)
        in_specs.append(pl.BlockSpec((1, D_MODEL), lambda i, j: (0, 0)))
        out_specs, out_shape = x_spec, x_shape
    else:
        ada_w, ada_b, cond = operands_ada = ada
        batch = cond.shape[0]
        blocks_per_row = 3 * D_MODEL // ADA_FUSED_TK
        n_blocks = 2 * blocks_per_row
        assert n_blocks <= (n_tok // MLP_TM) * steps

        def ada_block(i, j):
            blk = jnp.minimum(i * steps + j, n_blocks - 1)
            return 2 * (layer + 1) + blk // blocks_per_row, 0, blk % blocks_per_row

        def next_mod_block(i, j):
            row, _, col = ada_block(i, j)
            return row - 2 * (layer + 1), 0, col

        operands = (x2d, mod, norm_g, w1, w2) + operands_ada
        in_specs += [
            pl.BlockSpec((None, D_MODEL, ADA_FUSED_TK), ada_block),
            pl.BlockSpec((None, 1, ADA_FUSED_TK), ada_block),
            pl.BlockSpec(cond.shape, lambda i, j: (0, 0, 0)),
        ]
        out_specs = (x_spec, pl.BlockSpec((None, batch, ADA_FUSED_TK), next_mod_block))
        out_shape = (x_shape, jax.ShapeDtypeStruct((2, batch, 3 * D_MODEL), jnp.float32))
    return pl.pallas_call(
        functools.partial(_mlp_kernel, last),
        out_shape=out_shape,
        grid=(n_tok // MLP_TM, steps),
        in_specs=in_specs,
        out_specs=out_specs,
        scratch_shapes=[pltpu.VMEM((MLP_TM, D_MODEL), jnp.bfloat16),
                        pltpu.VMEM((2, MLP_TM, D_MODEL), jnp.float32),
                        pltpu.SemaphoreType.DMA((2, MLP_TM // ROW_CHUNK))],
        compiler_params=_params("arbitrary", "arbitrary"),
        name="mlp_final" if last else "mlp",
    )(*operands)


def _proj_kernel(x_hbm, mod_ref, g_ref, wqk_ref, wvg_ref, cos_ref, sin_ref, oqk_ref, ovg_ref, h_ref, xbuf_ref, sems):
    j = pl.program_id(1)
    n_q = RET_QK_DIM // PROJ_QK_TN

    @pl.when(j == 0)
    def _():
        first_row = pl.multiple_of(pl.program_id(0) * PROJ_TM, PROJ_TM)
        copies = _row_chunk_copies(x_hbm, first_row, lambda k: xbuf_ref.at[k], sems)
        for cp in copies:
            cp.start()
        for k, cp in enumerate(copies):
            cp.wait()
            rows = pl.ds(k * ROW_CHUNK, ROW_CHUNK)
            h_ref[rows, :] = _norm_modulate(xbuf_ref[k], g_ref[0], mod_ref[0]).astype(jnp.bfloat16)

    w = jnp.concatenate([wqk_ref[...].astype(jnp.bfloat16), wvg_ref[...].astype(jnp.bfloat16)], axis=1)
    r = jnp.dot(h_ref[...], w, preferred_element_type=jnp.float32)

    k_scale = jnp.where(j >= n_q, RET_HEAD_QK ** -0.5, 1.0)
    cos = cos_ref[...] * k_scale
    sin = sin_ref[...] * k_scale
    swap = lax.broadcasted_iota(jnp.int32, (PROJ_TM, LANES), 1) ^ 1
    for lo in range(0, PROJ_QK_TN, LANES):
        t = lo % RET_HEAD_QK
        xs = r[:, lo:lo + LANES]
        partner = jnp.take_along_axis(xs, swap, axis=1)
        oqk_ref[lo // PROJ_GROUP, :, t:t + LANES] = (
            xs * cos[:, t:t + LANES] + partner * sin[:, t:t + LANES]).astype(oqk_ref.dtype)
    for grp in range(PROJ_VG_TN // PROJ_GROUP):
        lo = PROJ_QK_TN + grp * PROJ_GROUP
        ovg_ref[grp] = r[:, lo:lo + PROJ_GROUP].astype(ovg_ref.dtype)


def _ret_projection(x2d, mod, mod_idx, seq, norm_g, norm_idx, w_in, layer, cos, sin):
    n_tok = x2d.shape[0]
    tiles_per_seq = seq // PROJ_TM
    steps = 2 * RET_QK_DIM // PROJ_QK_TN
    assert steps == 2 * RET_V_DIM // PROJ_VG_TN
    vg_off = 2 * RET_QK_DIM // PROJ_VG_TN
    return pl.pallas_call(
        _proj_kernel,
        out_shape=(jax.ShapeDtypeStruct((2 * RET_QK_DIM // PROJ_GROUP, n_tok, PROJ_GROUP), jnp.bfloat16),
                   jax.ShapeDtypeStruct((2 * RET_V_DIM // PROJ_GROUP, n_tok, PROJ_GROUP), jnp.bfloat16)),
        grid=(n_tok // PROJ_TM, steps),
        in_specs=[
            pl.BlockSpec(memory_space=pl.ANY),
            pl.BlockSpec((1, 1, 3 * D_MODEL), lambda i, j: (mod_idx + i // tiles_per_seq, 0, 0)),
            pl.BlockSpec((1, 1, D_MODEL), lambda i, j: (norm_idx, 0, 0)),
            pl.BlockSpec((None, D_MODEL, PROJ_QK_TN), lambda i, j: (layer, 0, j)),
            pl.BlockSpec((None, D_MODEL, PROJ_VG_TN), lambda i, j: (layer, 0, vg_off + j)),
            pl.BlockSpec((PROJ_TM, RET_HEAD_QK), lambda i, j: (i % tiles_per_seq, 0)),
            pl.BlockSpec((PROJ_TM, RET_HEAD_QK), lambda i, j: (i % tiles_per_seq, 0)),
        ],
        out_specs=(pl.BlockSpec((PROJ_QK_TN // PROJ_GROUP, PROJ_TM, PROJ_GROUP), lambda i, j: (j, i, 0)),
                   pl.BlockSpec((PROJ_VG_TN // PROJ_GROUP, PROJ_TM, PROJ_GROUP), lambda i, j: (j, i, 0))),
        scratch_shapes=[pltpu.VMEM((PROJ_TM, D_MODEL), jnp.bfloat16),
                        pltpu.VMEM((PROJ_TM // ROW_CHUNK, ROW_CHUNK, D_MODEL), jnp.float32),
                        pltpu.SemaphoreType.DMA((PROJ_TM // ROW_CHUNK,))],
        compiler_params=_params("parallel", "arbitrary"),
        name="ret_projection",
    )(x2d, mod, norm_g, w_in, w_in, cos, sin)


def _scan_kernel(q_ref, k_ref, v_ref, g_ref, mask_ref, eps_ref, kdec_ref, cdec_ref, o_ref, state_ref):
    @pl.when(pl.program_id(2) == 0)
    def _():
        state_ref[...] = jnp.zeros_like(state_ref)

    mask = mask_ref[...]
    row_eps = eps_ref[...]
    k_dec = kdec_ref[...]
    chunk_dec = cdec_ref[...]
    state = state_ref[...]
    for c in range(q_ref.shape[0] // RET_CHUNK):
        rows = slice(c * RET_CHUNK, (c + 1) * RET_CHUNK)
        qc = q_ref[rows, :]
        kc = k_ref[rows, :]
        vc = jnp.concatenate([v_ref[grp, rows, :] for grp in range(v_ref.shape[0])], axis=1)
        gc = jnp.concatenate([g_ref[grp, rows, :] for grp in range(g_ref.shape[0])], axis=1)
        scores = lax.dot_general(qc, kc, (((1,), (1,)), ((), ())), preferred_element_type=jnp.float32)
        scores = (scores * mask).astype(jnp.bfloat16)
        y = (jnp.dot(scores, vc, preferred_element_type=jnp.float32)
             + jnp.dot(qc, state.astype(jnp.bfloat16), preferred_element_type=jnp.float32))
        k_decayed = (kc.astype(jnp.float32) * k_dec).astype(jnp.bfloat16)
        state = chunk_dec * state + lax.dot_general(
            k_decayed, vc, (((0,), (0,)), ((), ())), preferred_element_type=jnp.float32)
        mu = jnp.mean(y, axis=-1, keepdims=True)
        yc = y - mu
        var = jnp.mean(yc * yc, axis=-1, keepdims=True)
        yn = yc * lax.rsqrt(var + row_eps)
        z = (jax.nn.silu(gc.astype(jnp.float32)) * yn).astype(o_ref.dtype)
        for grp in range(o_ref.shape[0]):
            o_ref[grp, rows, :] = z[:, grp * PROJ_GROUP:(grp + 1) * PROJ_GROUP]
    state_ref[...] = state


def _ret_scan(proj_qk, proj_vg, batch, seq, mask, row_eps, k_dec, chunk_dec):
    assert RET_HEAD_QK == PROJ_GROUP
    n_tok = proj_qk.shape[1]
    steps = seq // RET_TS
    v_groups = RET_HEAD_V // PROJ_GROUP
    k_off = RET_HEADS
    v_off = 0
    g_off = RET_HEADS
    return pl.pallas_call(
        _scan_kernel,
        out_shape=jax.ShapeDtypeStruct((RET_V_DIM // PROJ_GROUP, n_tok, PROJ_GROUP), jnp.bfloat16),
        grid=(batch, RET_HEADS, steps),
        in_specs=[
            pl.BlockSpec((None, RET_TS, PROJ_GROUP), lambda b, h, t: (h, b * steps + t, 0)),
            pl.BlockSpec((None, RET_TS, PROJ_GROUP), lambda b, h, t: (k_off + h, b * steps + t, 0)),
            pl.BlockSpec((v_groups, RET_TS, PROJ_GROUP), lambda b, h, t: (v_off + h, b * steps + t, 0)),
            pl.BlockSpec((v_groups, RET_TS, PROJ_GROUP), lambda b, h, t: (g_off + h, b * steps + t, 0)),
            pl.BlockSpec((None, RET_CHUNK, RET_CHUNK), lambda b, h, t: (h, 0, 0)),
            pl.BlockSpec((None, RET_CHUNK, 1), lambda b, h, t: (h, 0, 0)),
            pl.BlockSpec((None, RET_CHUNK, 1), lambda b, h, t: (h, 0, 0)),
            pl.BlockSpec((None, 1, 1), lambda b, h, t: (h, 0, 0)),
        ],
        out_specs=pl.BlockSpec((v_groups, RET_TS, PROJ_GROUP), lambda b, h, t: (h, b * steps + t, 0)),
        scratch_shapes=[pltpu.VMEM((RET_HEAD_QK, RET_HEAD_V), jnp.float32)],
        compiler_params=_params("parallel", "parallel", "arbitrary"),
        name="ret_scan",
    )(proj_qk, proj_qk, proj_vg, proj_vg, mask, row_eps, k_dec, chunk_dec)


def _out_kernel(z_ref, w_ref, x_ref, gate_ref, o_ref, wb_ref):
    @pl.when(pl.program_id(1) == 0)
    def _():
        wb_ref[...] = w_ref[...].astype(jnp.bfloat16)

    y = None
    for grp in range(z_ref.shape[0]):
        part = jnp.dot(z_ref[grp], wb_ref[grp * PROJ_GROUP:(grp + 1) * PROJ_GROUP, :],
                       preferred_element_type=jnp.float32)
        y = part if y is None else y + part
    o_ref[...] = x_ref[...] + gate_ref[0] * y


def _ret_output(z, x2d, mod, mod_idx, seq, w_out, layer):
    n_tok = x2d.shape[0]
    tiles_per_seq = seq // OUT_TM
    gate_off = 2 * D_MODEL // OUT_TN
    return pl.pallas_call(
        _out_kernel,
        out_shape=jax.ShapeDtypeStruct(x2d.shape, jnp.float32),
        grid=(D_MODEL // OUT_TN, n_tok // OUT_TM),
        in_specs=[
            pl.BlockSpec((RET_V_DIM // PROJ_GROUP, OUT_TM, PROJ_GROUP), lambda j, i: (0, i, 0)),
            pl.BlockSpec((None, RET_V_DIM, OUT_TN), lambda j, i: (layer, 0, j)),
            pl.BlockSpec((OUT_TM, OUT_TN), lambda j, i: (i, j)),
            pl.BlockSpec((1, 1, OUT_TN), lambda j, i: (mod_idx + i // tiles_per_seq, 0, gate_off + j)),
        ],
        out_specs=pl.BlockSpec((OUT_TM, OUT_TN), lambda j, i: (i, j)),
        scratch_shapes=[pltpu.VMEM((RET_V_DIM, OUT_TN), jnp.bfloat16)],
        compiler_params=_params("parallel", "arbitrary"),
        name="ret_output",
    )(z, w_out, x2d, mod)


def _rotary_tables(seq):
    inv = ROPE_BASE ** (-np.arange(0, RET_HEAD_QK, 2, dtype=np.float64) / RET_HEAD_QK)
    ang = np.arange(seq, dtype=np.float64)[:, None] * inv[None, :]
    cos = np.repeat(np.cos(ang), 2, axis=1)
    sin = np.stack([-np.sin(ang), np.sin(ang)], axis=-1).reshape(seq, RET_HEAD_QK)
    return cos.astype(np.float32), sin.astype(np.float32)


def _decay_tables():
    gamma = 1.0 - 2.0 ** (-5.0 - np.arange(RET_HEADS, dtype=np.float64))
    log_g = np.log(gamma)
    j = np.arange(RET_CHUNK, dtype=np.float64)
    causal = j[:, None] >= j[None, :]
    mask = np.where(causal[None], np.exp(-(j[None, None, :] + 1.0) * log_g[:, None, None]), 0.0)
    row_eps = (EPS * np.exp(-2.0 * (j[None, :] + 1.0) * log_g[:, None]))[:, :, None]
    k_dec = np.exp((RET_CHUNK - 1.0 - j[None, :]) * log_g[:, None])[:, :, None]
    chunk_dec = np.exp(RET_CHUNK * log_g)[:, None, None]
    return tuple(t.astype(np.float32) for t in (mask, row_eps, k_dec, chunk_dec))


@jax.jit
def kernel(x, c, ada_w, ada_b, norm_g, pool_w, pool_scale, ret_w_in, ret_w_out, mlp_w1, mlp_w2, final_g):
    batch, seq, d = x.shape
    assert d == D_MODEL
    assert seq % max(POOL_TM, MLP_TM, PROJ_TM, OUT_TM, RET_TS) == 0

    ada_w = ada_w.reshape(DEPTH * 2, d, 3 * d)
    ada_b = ada_b.reshape(DEPTH * 2, 1, 3 * d)
    c_lanes = jnp.broadcast_to(c[:, :, None], (batch, d, LANES))
    mod, cond = _ada_first_layer(c_lanes, ada_w, ada_b)

    cos, sin = _rotary_tables(seq)
    mask, row_eps, k_dec, chunk_dec = _decay_tables()
    norm_rows = norm_g.reshape(DEPTH * 2, 1, d)
    scale_rows = pool_scale.reshape(-1, 1, d)

    x2d = x.reshape(batch * seq, d)
    for i in range(DEPTH):
        mod = mod.reshape(2 * batch, 1, 3 * d)
        mix_idx, mlp_idx = 0, batch
        if i % 2 == 0:
            x2d = _pool_layer(x2d, mod, mix_idx, seq, norm_rows, 2 * i, pool_w, scale_rows, i // 2)
        else:
            proj_qk, proj_vg = _ret_projection(x2d, mod, mix_idx, seq, norm_rows, 2 * i, ret_w_in, i // 2, cos, sin)
            z = _ret_scan(proj_qk, proj_vg, batch, seq, mask, row_eps, k_dec, chunk_dec)
            x2d = _ret_output(z, x2d, mod, mix_idx, seq, ret_w_out, i // 2)
        if i < DEPTH - 1:
            x2d, mod = _mlp_layer(x2d, mod, mlp_idx, seq, norm_rows, 2 * i + 1, mlp_w1, mlp_w2, i,
                                  ada=(ada_w, ada_b, cond))
        else:
            x2d = _mlp_layer(x2d, mod, mlp_idx, seq, norm_rows, 2 * i + 1, mlp_w1, mlp_w2, i,
                             final_g=final_g.reshape(1, d))
    return x2d.reshape(batch, seq, d)
```

```python
import functools

import jax
import jax.numpy as jnp
import numpy as np
from jax import lax
from jax.experimental import pallas as pl
from jax.experimental.pallas import tpu as pltpu

D_MODEL = 2048
DEPTH = 4
POOL_WINDOWS = (2, 4, 8, 16)
POOL_GROUP = D_MODEL // len(POOL_WINDOWS)
POOL_HALO = 16
RET_HEADS = 8
RET_QK_DIM = D_MODEL
RET_V_DIM = 2 * D_MODEL
RET_HEAD_QK = RET_QK_DIM // RET_HEADS
RET_HEAD_V = RET_V_DIM // RET_HEADS
RET_PROJ = 2 * RET_QK_DIM + 2 * RET_V_DIM
ROPE_BASE = 10000.0
D_FF = 4 * D_MODEL
EPS = 1e-6
LANES = 128

V7X_VMEM_BYTES = 64 * 1024 * 1024
VMEM_LIMIT_BYTES = V7X_VMEM_BYTES - 4 * 1024 * 1024

ROW_CHUNK = 128
NORM_COLS = 256
ADA_TK = 1024
ADA_FUSED_TK = 128
POOL_TM = 512
MLP_TM = 1024
MLP_TF = 512
MLP_TN = 512
PROJ_TM = 1024
PROJ_QK_TN = 512
PROJ_VG_TN = 1024
PROJ_GROUP = 256
RET_TS = 1024
RET_CHUNK = 256
OUT_TM = 1024
OUT_TN = 512


def _params(*semantics):
    return pltpu.CompilerParams(dimension_semantics=semantics, vmem_limit_bytes=VMEM_LIMIT_BYTES)


def _norm_modulate(x, g, mod_row):
    shift = mod_row[:, 0:D_MODEL]
    scale = mod_row[:, D_MODEL:2 * D_MODEL]
    inv = lax.rsqrt(jnp.mean(x * x, axis=-1, keepdims=True) + EPS)
    return x * inv * (g * (1.0 + scale)) + shift


def _ada_columns(w, cond_ref, bias):
    rows = []
    for b in range(cond_ref.shape[0]):
        cond = cond_ref[b]
        rows.append(jnp.concatenate(
            [jnp.sum(w[:, lo:lo + LANES] * cond, axis=0, keepdims=True) for lo in range(0, w.shape[1], LANES)],
            axis=1))
    return jnp.concatenate(rows, axis=0) + bias


def _ada_kernel(c_ref, w_ref, b_ref, o_ref, cond_ref):
    cond_ref[...] = jax.nn.silu(c_ref[...])
    o_ref[...] = _ada_columns(w_ref[...], cond_ref, b_ref[...])


def _ada_first_layer(c_lanes, ada_w, ada_b):
    batch = c_lanes.shape[0]
    n_out = ada_w.shape[2]
    return pl.pallas_call(
        _ada_kernel,
        out_shape=(jax.ShapeDtypeStruct((2, batch, n_out), jnp.float32),
                   jax.ShapeDtypeStruct(c_lanes.shape, jnp.float32)),
        grid=(2, n_out // ADA_TK),
        in_specs=[
            pl.BlockSpec(c_lanes.shape, lambda l, j: (0, 0, 0)),
            pl.BlockSpec((None, D_MODEL, ADA_TK), lambda l, j: (l, 0, j)),
            pl.BlockSpec((None, 1, ADA_TK), lambda l, j: (l, 0, j)),
        ],
        out_specs=(pl.BlockSpec((None, batch, ADA_TK), lambda l, j: (l, 0, j)),
                   pl.BlockSpec(c_lanes.shape, lambda l, j: (0, 0, 0))),
        compiler_params=_params("arbitrary", "arbitrary"),
        name="ada_modulation",
    )(c_lanes, ada_w, ada_b)


def _pool_kernel(tiles_per_seq, x_ref, halo_ref, mod_ref, g_ref, w_ref, ls_ref, o_ref, wb_ref):
    i = pl.program_id(0)

    @pl.when(i == 0)
    def _():
        wb_ref[...] = w_ref[...].astype(jnp.bfloat16)

    tile_in_seq = i % tiles_per_seq
    tm = x_ref.shape[0]
    mod_row = mod_ref[0]
    g = g_ref[0]
    x = x_ref[...]
    h = _norm_modulate(x, g, mod_row)
    h_halo = _norm_modulate(halo_ref[...], g, mod_row)
    h_halo = jnp.where(tile_in_seq == 0, 0.0, h_halo)
    out_scale = mod_row[:, 2 * D_MODEL:3 * D_MODEL] * ls_ref[0]
    pos = tile_in_seq * tm + lax.broadcasted_iota(jnp.int32, (tm, 1), 0)
    for gi, window in enumerate(POOL_WINDOWS):
        cols = slice(gi * POOL_GROUP, (gi + 1) * POOL_GROUP)
        hg = h[:, cols]
        acc = jnp.concatenate([h_halo[:, cols], hg], axis=0)
        shift = 1
        while shift < window:
            acc = acc + pltpu.roll(acc, shift, axis=0)
            shift *= 2
        cnt = jnp.minimum(pos + 1, window).astype(jnp.float32)
        p = acc[POOL_HALO:, :] / cnt - hg
        y = jnp.dot(p.astype(jnp.bfloat16), wb_ref[gi], preferred_element_type=jnp.float32)
        o_ref[:, cols] = x[:, cols] + out_scale[:, cols] * y


def _pool_layer(x2d, mod, mod_idx, seq, norm_g, norm_idx, pool_w, pool_scale, pool_idx):
    n_tok = x2d.shape[0]
    tiles_per_seq = seq // POOL_TM
    halo_blocks_per_tile = POOL_TM // POOL_HALO
    n_grp = len(POOL_WINDOWS)
    return pl.pallas_call(
        functools.partial(_pool_kernel, tiles_per_seq),
        out_shape=jax.ShapeDtypeStruct(x2d.shape, jnp.float32),
        grid=(n_tok // POOL_TM,),
        in_specs=[
            pl.BlockSpec((POOL_TM, D_MODEL), lambda i: (i, 0)),
            pl.BlockSpec((POOL_HALO, D_MODEL), lambda i: (jnp.maximum(i * halo_blocks_per_tile - 1, 0), 0)),
            pl.BlockSpec((1, 1, 3 * D_MODEL), lambda i: (mod_idx + i // tiles_per_seq, 0, 0)),
            pl.BlockSpec((1, 1, D_MODEL), lambda i: (norm_idx, 0, 0)),
            pl.BlockSpec((None, n_grp, POOL_GROUP, POOL_GROUP), lambda i: (pool_idx, 0, 0, 0)),
            pl.BlockSpec((1, 1, D_MODEL), lambda i: (pool_idx, 0, 0)),
        ],
        out_specs=pl.BlockSpec((POOL_TM, D_MODEL), lambda i: (i, 0)),
        scratch_shapes=[pltpu.VMEM((n_grp, POOL_GROUP, POOL_GROUP), jnp.bfloat16)],
        compiler_params=_params("arbitrary"),
        name="pool_mixer",
    )(x2d, x2d, mod, norm_g, pool_w, pool_scale)


def _tile_copies(x_hbm, x_ref, sems, tile):
    slots, tm, _ = x_ref.shape
    slot = tile % slots
    first_row = pl.multiple_of(tile * tm, tm)
    return [
        pltpu.make_async_copy(x_hbm.at[pl.ds(first_row + k * ROW_CHUNK, ROW_CHUNK), :],
                              x_ref.at[slot, pl.ds(k * ROW_CHUNK, ROW_CHUNK), :], sems.at[slot, k])
        for k in range(tm // ROW_CHUNK)
    ]


def _request_tile(x_hbm, x_ref, sems, tile, when):
    @pl.when(when)
    def _():
        for cp in _tile_copies(x_hbm, x_ref, sems, tile):
            cp.start()


def _await_tile(x_hbm, x_ref, sems, tile):
    for cp in _tile_copies(x_hbm, x_ref, sems, tile):
        cp.wait()
    return x_ref.at[tile % x_ref.shape[0]]


def _normalise_into_first_product(x_tile, g, mod_row, h_ref, weight_rows):
    x = x_tile[...]
    inv = lax.rsqrt(jnp.mean(x * x, axis=-1, keepdims=True) + EPS)
    col_gain = g * (1.0 + mod_row[:, D_MODEL:2 * D_MODEL])
    acc = None
    for lo in range(0, D_MODEL, NORM_COLS):
        cols = slice(lo, lo + NORM_COLS)
        hk = (x_tile[:, cols] * inv * col_gain[:, cols] + mod_row[:, cols]).astype(jnp.bfloat16)
        h_ref[:, cols] = hk
        part = jnp.dot(hk, weight_rows(cols), preferred_element_type=jnp.float32)
        acc = part if acc is None else acc + part
    return acc


def _mlp_kernel(last, x_hbm, mod_ref, g_ref, w1_ref, w2_ref, *refs):
    if last:
        fg_ref, o_ref, h_ref, x_ref, sems = refs
    else:
        wa_ref, ba_ref, cond_ref, o_ref, next_mod_ref, h_ref, x_ref, sems = refs
    i = pl.program_id(0)
    j = pl.program_id(1)

    def finish(pre, base_ref):
        a = jnp.maximum(pre, 0.0)
        a = (a * a).astype(jnp.bfloat16)
        w2 = w2_ref[...].astype(jnp.bfloat16)
        for n in range(0, D_MODEL, MLP_TN):
            cols = slice(n, n + MLP_TN)
            gate = mod_ref[0, :, 2 * D_MODEL + n:2 * D_MODEL + n + MLP_TN]
            o_ref[:, cols] = base_ref[:, cols] + gate * jnp.dot(a, w2[:, cols], preferred_element_type=jnp.float32)
        if not last:
            next_mod_ref[...] = _ada_columns(wa_ref[...], cond_ref, ba_ref[...])

    @pl.when(j == 0)
    def _():
        _request_tile(x_hbm, x_ref, sems, i, i == 0)
        _request_tile(x_hbm, x_ref, sems, i + 1, i + 1 < pl.num_programs(0))
        x_tile = _await_tile(x_hbm, x_ref, sems, i)
        pre = _normalise_into_first_product(x_tile, g_ref[0], mod_ref[0], h_ref,
                                            lambda rows: w1_ref[rows, :].astype(jnp.bfloat16))
        finish(pre, x_tile)

    @pl.when(j > 0)
    def _():
        finish(jnp.dot(h_ref[...], w1_ref[...].astype(jnp.bfloat16), preferred_element_type=jnp.float32), o_ref)

    if last:
        @pl.when(j == pl.num_programs(1) - 1)
        def _():
            y = o_ref[...]
            inv = lax.rsqrt(jnp.mean(y * y, axis=-1, keepdims=True) + EPS)
            o_ref[...] = y * inv * fg_ref[...]


def _mlp_layer(x2d, mod, mod_idx, seq, norm_g, norm_idx, w1, w2, layer, final_g=None, ada=None):
    n_tok = x2d.shape[0]
    tiles_per_seq = seq // MLP_TM
    steps = D_FF // MLP_TF
    last = ada is None
    in_specs = [
        pl.BlockSpec(memory_space=pl.ANY),
        pl.BlockSpec((1, 1, 3 * D_MODEL), lambda i, j: (mod_idx + i // tiles_per_seq, 0, 0)),
        pl.BlockSpec((1, 1, D_MODEL), lambda i, j: (norm_idx, 0, 0)),
        pl.BlockSpec((None, D_MODEL, MLP_TF), lambda i, j: (layer, 0, j)),
        pl.BlockSpec((None, MLP_TF, D_MODEL), lambda i, j: (layer, j, 0)),
    ]
    x_spec = pl.BlockSpec((MLP_TM, D_MODEL), lambda i, j: (i, 0))
    x_shape = jax.ShapeDtypeStruct(x2d.shape, jnp.float32)
    if last:
        operands = (x2d, mod, norm_g, w1, w2, final_g)
        in_specs.append(pl.BlockSpec((1, D_MODEL), lambda i, j: (0, 0)))
        out_specs, out_shape = x_spec, x_shape
    else:
        ada_w, ada_b, cond = operands_ada = ada
        batch = cond.shape[0]
        blocks_per_row = 3 * D_MODEL // ADA_FUSED_TK
        n_blocks = 2 * blocks_per_row
        assert n_blocks <= (n_tok // MLP_TM) * steps

        def ada_block(i, j):
            blk = jnp.minimum(i * steps + j, n_blocks - 1)
            return 2 * (layer + 1) + blk // blocks_per_row, 0, blk % blocks_per_row

        def next_mod_block(i, j):
            row, _, col = ada_block(i, j)
            return row - 2 * (layer + 1), 0, col

        operands = (x2d, mod, norm_g, w1, w2) + operands_ada
        in_specs += [
            pl.BlockSpec((None, D_MODEL, ADA_FUSED_TK), ada_block),
            pl.BlockSpec((None, 1, ADA_FUSED_TK), ada_block),
            pl.BlockSpec(cond.shape, lambda i, j: (0, 0, 0)),
        ]
        out_specs = (x_spec, pl.BlockSpec((None, batch, ADA_FUSED_TK), next_mod_block))
        out_shape = (x_shape, jax.ShapeDtypeStruct((2, batch, 3 * D_MODEL), jnp.float32))
    return pl.pallas_call(
        functools.partial(_mlp_kernel, last),
        out_shape=out_shape,
        grid=(n_tok // MLP_TM, steps),
        in_specs=in_specs,
        out_specs=out_specs,
        scratch_shapes=[pltpu.VMEM((MLP_TM, D_MODEL), jnp.bfloat16),
                        pltpu.VMEM((2, MLP_TM, D_MODEL), jnp.float32),
                        pltpu.SemaphoreType.DMA((2, MLP_TM // ROW_CHUNK))],
        compiler_params=_params("arbitrary", "arbitrary"),
        name="mlp_final" if last else "mlp",
    )(*operands)


def _proj_kernel(x_hbm, mod_ref, g_ref, wqk_ref, wvg_ref, cos_ref, sin_ref, oqk_ref, ovg_ref, h_ref, x_ref, sems):
    i = pl.program_id(0)
    j = pl.program_id(1)
    n_q = RET_QK_DIM // PROJ_QK_TN

    def weights(rows=slice(None)):
        return jnp.concatenate([wqk_ref[rows, :].astype(jnp.bfloat16), wvg_ref[rows, :].astype(jnp.bfloat16)], axis=1)

    def finish(r):
        k_scale = jnp.where(j >= n_q, RET_HEAD_QK ** -0.5, 1.0)
        cos = cos_ref[...] * k_scale
        sin = sin_ref[...] * k_scale
        swap = lax.broadcasted_iota(jnp.int32, (PROJ_TM, LANES), 1) ^ 1
        for lo in range(0, PROJ_QK_TN, LANES):
            t = lo % RET_HEAD_QK
            xs = r[:, lo:lo + LANES]
            partner = jnp.take_along_axis(xs, swap, axis=1)
            oqk_ref[lo // PROJ_GROUP, :, t:t + LANES] = (
                xs * cos[:, t:t + LANES] + partner * sin[:, t:t + LANES]).astype(oqk_ref.dtype)
        for grp in range(PROJ_VG_TN // PROJ_GROUP):
            lo = PROJ_QK_TN + grp * PROJ_GROUP
            ovg_ref[grp] = r[:, lo:lo + PROJ_GROUP].astype(ovg_ref.dtype)

    @pl.when(j == 0)
    def _():
        _request_tile(x_hbm, x_ref, sems, i, i == 0)
        x_tile = _await_tile(x_hbm, x_ref, sems, i)
        finish(_normalise_into_first_product(x_tile, g_ref[0], mod_ref[0], h_ref, weights))

    _request_tile(x_hbm, x_ref, sems, i + 1, (j == 1) & (i + 1 < pl.num_programs(0)))

    @pl.when(j > 0)
    def _():
        finish(jnp.dot(h_ref[...], weights(), preferred_element_type=jnp.float32))


def _ret_projection(x2d, mod, mod_idx, seq, norm_g, norm_idx, w_in, layer, cos, sin):
    n_tok = x2d.shape[0]
    tiles_per_seq = seq // PROJ_TM
    steps = 2 * RET_QK_DIM // PROJ_QK_TN
    assert steps == 2 * RET_V_DIM // PROJ_VG_TN
    vg_off = 2 * RET_QK_DIM // PROJ_VG_TN
    return pl.pallas_call(
        _proj_kernel,
        out_shape=(jax.ShapeDtypeStruct((2 * RET_QK_DIM // PROJ_GROUP, n_tok, PROJ_GROUP), jnp.bfloat16),
                   jax.ShapeDtypeStruct((2 * RET_V_DIM // PROJ_GROUP, n_tok, PROJ_GROUP), jnp.bfloat16)),
        grid=(n_tok // PROJ_TM, steps),
        in_specs=[
            pl.BlockSpec(memory_space=pl.ANY),
            pl.BlockSpec((1, 1, 3 * D_MODEL), lambda i, j: (mod_idx + i // tiles_per_seq, 0, 0)),
            pl.BlockSpec((1, 1, D_MODEL), lambda i, j: (norm_idx, 0, 0)),
            pl.BlockSpec((None, D_MODEL, PROJ_QK_TN), lambda i, j: (layer, 0, j)),
            pl.BlockSpec((None, D_MODEL, PROJ_VG_TN), lambda i, j: (layer, 0, vg_off + j)),
            pl.BlockSpec((PROJ_TM, RET_HEAD_QK), lambda i, j: (i % tiles_per_seq, 0)),
            pl.BlockSpec((PROJ_TM, RET_HEAD_QK), lambda i, j: (i % tiles_per_seq, 0)),
        ],
        out_specs=(pl.BlockSpec((PROJ_QK_TN // PROJ_GROUP, PROJ_TM, PROJ_GROUP), lambda i, j: (j, i, 0)),
                   pl.BlockSpec((PROJ_VG_TN // PROJ_GROUP, PROJ_TM, PROJ_GROUP), lambda i, j: (j, i, 0))),
        scratch_shapes=[pltpu.VMEM((PROJ_TM, D_MODEL), jnp.bfloat16),
                        pltpu.VMEM((1, PROJ_TM, D_MODEL), jnp.float32),
                        pltpu.SemaphoreType.DMA((1, PROJ_TM // ROW_CHUNK))],
        compiler_params=_params("arbitrary", "arbitrary"),
        name="ret_projection",
    )(x2d, mod, norm_g, w_in, w_in, cos, sin)


def _scan_kernel(q_ref, k_ref, v_ref, g_ref, mask_ref, eps_ref, kdec_ref, cdec_ref, o_ref, state_ref):
    @pl.when(pl.program_id(2) == 0)
    def _():
        state_ref[...] = jnp.zeros_like(state_ref)

    mask = mask_ref[...]
    row_eps = eps_ref[...]
    k_dec = kdec_ref[...]
    chunk_dec = cdec_ref[...]
    state = state_ref[...]
    for c in range(q_ref.shape[0] // RET_CHUNK):
        rows = slice(c * RET_CHUNK, (c + 1) * RET_CHUNK)
        qc = q_ref[rows, :]
        kc = k_ref[rows, :]
        vc = jnp.concatenate([v_ref[grp, rows, :] for grp in range(v_ref.shape[0])], axis=1)
        gc = jnp.concatenate([g_ref[grp, rows, :] for grp in range(g_ref.shape[0])], axis=1)
        scores = lax.dot_general(qc, kc, (((1,), (1,)), ((), ())), preferred_element_type=jnp.float32)
        scores = (scores * mask).astype(jnp.bfloat16)
        y = (jnp.dot(scores, vc, preferred_element_type=jnp.float32)
             + jnp.dot(qc, state.astype(jnp.bfloat16), preferred_element_type=jnp.float32))
        k_decayed = (kc.astype(jnp.float32) * k_dec).astype(jnp.bfloat16)
        state = chunk_dec * state + lax.dot_general(
            k_decayed, vc, (((0,), (0,)), ((), ())), preferred_element_type=jnp.float32)
        mu = jnp.mean(y, axis=-1, keepdims=True)
        yc = y - mu
        var = jnp.mean(yc * yc, axis=-1, keepdims=True)
        yn = yc * lax.rsqrt(var + row_eps)
        z = (jax.nn.silu(gc.astype(jnp.float32)) * yn).astype(o_ref.dtype)
        for grp in range(o_ref.shape[0]):
            o_ref[grp, rows, :] = z[:, grp * PROJ_GROUP:(grp + 1) * PROJ_GROUP]
    state_ref[...] = state


def _ret_scan(proj_qk, proj_vg, batch, seq, mask, row_eps, k_dec, chunk_dec):
    assert RET_HEAD_QK == PROJ_GROUP
    n_tok = proj_qk.shape[1]
    steps = seq // RET_TS
    v_groups = RET_HEAD_V // PROJ_GROUP
    k_off = RET_HEADS
    v_off = 0
    g_off = RET_HEADS
    return pl.pallas_call(
        _scan_kernel,
        out_shape=jax.ShapeDtypeStruct((RET_V_DIM // PROJ_GROUP, n_tok, PROJ_GROUP), jnp.bfloat16),
        grid=(batch, RET_HEADS, steps),
        in_specs=[
            pl.BlockSpec((None, RET_TS, PROJ_GROUP), lambda b, h, t: (h, b * steps + t, 0)),
            pl.BlockSpec((None, RET_TS, PROJ_GROUP), lambda b, h, t: (k_off + h, b * steps + t, 0)),
            pl.BlockSpec((v_groups, RET_TS, PROJ_GROUP), lambda b, h, t: (v_off + h, b * steps + t, 0)),
            pl.BlockSpec((v_groups, RET_TS, PROJ_GROUP), lambda b, h, t: (g_off + h, b * steps + t, 0)),
            pl.BlockSpec((None, RET_CHUNK, RET_CHUNK), lambda b, h, t: (h, 0, 0)),
            pl.BlockSpec((None, RET_CHUNK, 1), lambda b, h, t: (h, 0, 0)),
            pl.BlockSpec((None, RET_CHUNK, 1), lambda b, h, t: (h, 0, 0)),
            pl.BlockSpec((None, 1, 1), lambda b, h, t: (h, 0, 0)),
        ],
        out_specs=pl.BlockSpec((v_groups, RET_TS, PROJ_GROUP), lambda b, h, t: (h, b * steps + t, 0)),
        scratch_shapes=[pltpu.VMEM((RET_HEAD_QK, RET_HEAD_V), jnp.float32)],
        compiler_params=_params("parallel", "parallel", "arbitrary"),
        name="ret_scan",
    )(proj_qk, proj_qk, proj_vg, proj_vg, mask, row_eps, k_dec, chunk_dec)


def _out_kernel(z_ref, w_ref, x_ref, gate_ref, o_ref, wb_ref):
    @pl.when(pl.program_id(1) == 0)
    def _():
        wb_ref[...] = w_ref[...].astype(jnp.bfloat16)

    y = None
    for grp in range(z_ref.shape[0]):
        part = jnp.dot(z_ref[grp], wb_ref[grp * PROJ_GROUP:(grp + 1) * PROJ_GROUP, :],
                       preferred_element_type=jnp.float32)
        y = part if y is None else y + part
    o_ref[...] = x_ref[...] + gate_ref[0] * y


def _ret_output(z, x2d, mod, mod_idx, seq, w_out, layer):
    n_tok = x2d.shape[0]
    tiles_per_seq = seq // OUT_TM
    gate_off = 2 * D_MODEL // OUT_TN
    return pl.pallas_call(
        _out_kernel,
        out_shape=jax.ShapeDtypeStruct(x2d.shape, jnp.float32),
        grid=(D_MODEL // OUT_TN, n_tok // OUT_TM),
        in_specs=[
            pl.BlockSpec((RET_V_DIM // PROJ_GROUP, OUT_TM, PROJ_GROUP), lambda j, i: (0, i, 0)),
            pl.BlockSpec((None, RET_V_DIM, OUT_TN), lambda j, i: (layer, 0, j)),
            pl.BlockSpec((OUT_TM, OUT_TN), lambda j, i: (i, j)),
            pl.BlockSpec((1, 1, OUT_TN), lambda j, i: (mod_idx + i // tiles_per_seq, 0, gate_off + j)),
        ],
        out_specs=pl.BlockSpec((OUT_TM, OUT_TN), lambda j, i: (i, j)),
        scratch_shapes=[pltpu.VMEM((RET_V_DIM, OUT_TN), jnp.bfloat16)],
        compiler_params=_params("parallel", "arbitrary"),
        name="ret_output",
    )(z, w_out, x2d, mod)


def _rotary_tables(seq):
    inv = ROPE_BASE ** (-np.arange(0, RET_HEAD_QK, 2, dtype=np.float64) / RET_HEAD_QK)
    ang = np.arange(seq, dtype=np.float64)[:, None] * inv[None, :]
    cos = np.repeat(np.cos(ang), 2, axis=1)
    sin = np.stack([-np.sin(ang), np.sin(ang)], axis=-1).reshape(seq, RET_HEAD_QK)
    return cos.astype(np.float32), sin.astype(np.float32)


def _decay_tables():
    gamma = 1.0 - 2.0 ** (-5.0 - np.arange(RET_HEADS, dtype=np.float64))
    log_g = np.log(gamma)
    j = np.arange(RET_CHUNK, dtype=np.float64)
    causal = j[:, None] >= j[None, :]
    mask = np.where(causal[None], np.exp(-(j[None, None, :] + 1.0) * log_g[:, None, None]), 0.0)
    row_eps = (EPS * np.exp(-2.0 * (j[None, :] + 1.0) * log_g[:, None]))[:, :, None]
    k_dec = np.exp((RET_CHUNK - 1.0 - j[None, :]) * log_g[:, None])[:, :, None]
    chunk_dec = np.exp(RET_CHUNK * log_g)[:, None, None]
    return tuple(t.astype(np.float32) for t in (mask, row_eps, k_dec, chunk_dec))


@jax.jit
def kernel(x, c, ada_w, ada_b, norm_g, pool_w, pool_scale, ret_w_in, ret_w_out, mlp_w1, mlp_w2, final_g):
    batch, seq, d = x.shape
    assert d == D_MODEL
    assert seq % max(POOL_TM, MLP_TM, PROJ_TM, OUT_TM, RET_TS) == 0

    ada_w = ada_w.reshape(DEPTH * 2, d, 3 * d)
    ada_b = ada_b.reshape(DEPTH * 2, 1, 3 * d)
    c_lanes = jnp.broadcast_to(c[:, :, None], (batch, d, LANES))
    mod, cond = _ada_first_layer(c_lanes, ada_w, ada_b)

    cos, sin = _rotary_tables(seq)
    mask, row_eps, k_dec, chunk_dec = _decay_tables()
    norm_rows = norm_g.reshape(DEPTH * 2, 1, d)
    scale_rows = pool_scale.reshape(-1, 1, d)

    x2d = x.reshape(batch * seq, d)
    for i in range(DEPTH):
        mod = mod.reshape(2 * batch, 1, 3 * d)
        mix_idx, mlp_idx = 0, batch
        if i % 2 == 0:
            x2d = _pool_layer(x2d, mod, mix_idx, seq, norm_rows, 2 * i, pool_w, scale_rows, i // 2)
        else:
            proj_qk, proj_vg = _ret_projection(x2d, mod, mix_idx, seq, norm_rows, 2 * i, ret_w_in, i // 2, cos, sin)
            z = _ret_scan(proj_qk, proj_vg, batch, seq, mask, row_eps, k_dec, chunk_dec)
            x2d = _ret_output(z, x2d, mod, mix_idx, seq, ret_w_out, i // 2)
        if i < DEPTH - 1:
            x2d, mod = _mlp_layer(x2d, mod, mlp_idx, seq, norm_rows, 2 * i + 1, mlp_w1, mlp_w2, i,
                                  ada=(ada_w, ada_b, cond))
        else:
            x2d = _mlp_layer(x2d, mod, mlp_idx, seq, norm_rows, 2 * i + 1, mlp_w1, mlp_w2, i,
                             final_g=final_g.reshape(1, d))
    return x2d.reshape(batch, seq, d)
```

```python
import functools

import jax
import jax.numpy as jnp
import numpy as np
from jax import lax
from jax.experimental import pallas as pl
from jax.experimental.pallas import tpu as pltpu

D_MODEL = 2048
DEPTH = 4
POOL_WINDOWS = (2, 4, 8, 16)
POOL_GROUP = D_MODEL // len(POOL_WINDOWS)
POOL_HALO = 16
RET_HEADS = 8
RET_QK_DIM = D_MODEL
RET_V_DIM = 2 * D_MODEL
RET_HEAD_QK = RET_QK_DIM // RET_HEADS
RET_HEAD_V = RET_V_DIM // RET_HEADS
RET_PROJ = 2 * RET_QK_DIM + 2 * RET_V_DIM
ROPE_BASE = 10000.0
D_FF = 4 * D_MODEL
EPS = 1e-6
LANES = 128

V7X_VMEM_BYTES = 64 * 1024 * 1024
VMEM_LIMIT_BYTES = V7X_VMEM_BYTES - 4 * 1024 * 1024

ROW_CHUNK = 128
NORM_COLS = 256
ADA_TK = 1024
ADA_FUSED_TK = 128
POOL_TM = 1024
MLP_TM = 1024
MLP_TF = 512
MLP_TN = 512
PROJ_TM = 1024
PROJ_QK_TN = 512
PROJ_VG_TN = 1024
PROJ_GROUP = 256
RET_TS = 2048
RET_CHUNK = 256
OUT_TM = 1024
OUT_TN = 512


def _params(*semantics):
    return pltpu.CompilerParams(dimension_semantics=semantics, vmem_limit_bytes=VMEM_LIMIT_BYTES)


def _norm_modulate(x, g, mod_row):
    shift = mod_row[:, 0:D_MODEL]
    scale = mod_row[:, D_MODEL:2 * D_MODEL]
    inv = lax.rsqrt(jnp.mean(x * x, axis=-1, keepdims=True) + EPS)
    return x * inv * (g * (1.0 + scale)) + shift


def _ada_columns(w, cond_ref, bias):
    rows = []
    for b in range(cond_ref.shape[0]):
        cond = cond_ref[b]
        rows.append(jnp.concatenate(
            [jnp.sum(w[:, lo:lo + LANES] * cond, axis=0, keepdims=True) for lo in range(0, w.shape[1], LANES)],
            axis=1))
    return jnp.concatenate(rows, axis=0) + bias


def _ada_kernel(c_ref, w_ref, b_ref, o_ref, cond_ref):
    cond_ref[...] = jax.nn.silu(c_ref[...])
    o_ref[...] = _ada_columns(w_ref[...], cond_ref, b_ref[...])


def _ada_first_layer(c_lanes, ada_w, ada_b):
    batch = c_lanes.shape[0]
    n_out = ada_w.shape[2]
    return pl.pallas_call(
        _ada_kernel,
        out_shape=(jax.ShapeDtypeStruct((2, batch, n_out), jnp.float32),
                   jax.ShapeDtypeStruct(c_lanes.shape, jnp.float32)),
        grid=(2, n_out // ADA_TK),
        in_specs=[
            pl.BlockSpec(c_lanes.shape, lambda l, j: (0, 0, 0)),
            pl.BlockSpec((None, D_MODEL, ADA_TK), lambda l, j: (l, 0, j)),
            pl.BlockSpec((None, 1, ADA_TK), lambda l, j: (l, 0, j)),
        ],
        out_specs=(pl.BlockSpec((None, batch, ADA_TK), lambda l, j: (l, 0, j)),
                   pl.BlockSpec(c_lanes.shape, lambda l, j: (0, 0, 0))),
        compiler_params=_params("arbitrary", "arbitrary"),
        name="ada_modulation",
    )(c_lanes, ada_w, ada_b)


def _pool_kernel(tiles_per_seq, x_ref, halo_ref, mod_ref, g_ref, w_ref, ls_ref, o_ref, wb_ref):
    i = pl.program_id(0)

    @pl.when(i == 0)
    def _():
        wb_ref[...] = w_ref[...].astype(jnp.bfloat16)

    tile_in_seq = i % tiles_per_seq
    tm = x_ref.shape[0]
    mod_row = mod_ref[0]
    g = g_ref[0]
    x = x_ref[...]
    h = _norm_modulate(x, g, mod_row)
    h_halo = _norm_modulate(halo_ref[...], g, mod_row)
    h_halo = jnp.where(tile_in_seq == 0, 0.0, h_halo)
    out_scale = mod_row[:, 2 * D_MODEL:3 * D_MODEL] * ls_ref[0]
    pos = tile_in_seq * tm + lax.broadcasted_iota(jnp.int32, (tm, 1), 0)
    for gi, window in enumerate(POOL_WINDOWS):
        cols = slice(gi * POOL_GROUP, (gi + 1) * POOL_GROUP)
        hg = h[:, cols]
        acc = jnp.concatenate([h_halo[:, cols], hg], axis=0)
        shift = 1
        while shift < window:
            acc = acc + pltpu.roll(acc, shift, axis=0)
            shift *= 2
        cnt = jnp.minimum(pos + 1, window).astype(jnp.float32)
        p = acc[POOL_HALO:, :] / cnt - hg
        y = jnp.dot(p.astype(jnp.bfloat16), wb_ref[gi], preferred_element_type=jnp.float32)
        o_ref[:, cols] = x[:, cols] + out_scale[:, cols] * y


def _pool_layer(x2d, mod, mod_idx, seq, norm_g, norm_idx, pool_w, pool_scale, pool_idx):
    n_tok = x2d.shape[0]
    tiles_per_seq = seq // POOL_TM
    halo_blocks_per_tile = POOL_TM // POOL_HALO
    n_grp = len(POOL_WINDOWS)
    return pl.pallas_call(
        functools.partial(_pool_kernel, tiles_per_seq),
        out_shape=jax.ShapeDtypeStruct(x2d.shape, jnp.float32),
        grid=(n_tok // POOL_TM,),
        in_specs=[
            pl.BlockSpec((POOL_TM, D_MODEL), lambda i: (i, 0)),
            pl.BlockSpec((POOL_HALO, D_MODEL), lambda i: (jnp.maximum(i * halo_blocks_per_tile - 1, 0), 0)),
            pl.BlockSpec((1, 1, 3 * D_MODEL), lambda i: (mod_idx + i // tiles_per_seq, 0, 0)),
            pl.BlockSpec((1, 1, D_MODEL), lambda i: (norm_idx, 0, 0)),
            pl.BlockSpec((None, n_grp, POOL_GROUP, POOL_GROUP), lambda i: (pool_idx, 0, 0, 0)),
            pl.BlockSpec((1, 1, D_MODEL), lambda i: (pool_idx, 0, 0)),
        ],
        out_specs=pl.BlockSpec((POOL_TM, D_MODEL), lambda i: (i, 0)),
        scratch_shapes=[pltpu.VMEM((n_grp, POOL_GROUP, POOL_GROUP), jnp.bfloat16)],
        compiler_params=_params("arbitrary"),
        name="pool_mixer",
    )(x2d, x2d, mod, norm_g, pool_w, pool_scale)


def _tile_copies(x_hbm, x_ref, sems, tile):
    slots, tm, _ = x_ref.shape
    slot = tile % slots
    first_row = pl.multiple_of(tile * tm, tm)
    return [
        pltpu.make_async_copy(x_hbm.at[pl.ds(first_row + k * ROW_CHUNK, ROW_CHUNK), :],
                              x_ref.at[slot, pl.ds(k * ROW_CHUNK, ROW_CHUNK), :], sems.at[slot, k])
        for k in range(tm // ROW_CHUNK)
    ]


def _request_tile(x_hbm, x_ref, sems, tile, when):
    @pl.when(when)
    def _():
        for cp in _tile_copies(x_hbm, x_ref, sems, tile):
            cp.start()


def _await_tile(x_hbm, x_ref, sems, tile):
    for cp in _tile_copies(x_hbm, x_ref, sems, tile):
        cp.wait()
    return x_ref.at[tile % x_ref.shape[0]]


def _normalise_into_first_product(x_tile, g, mod_row, h_ref, weight_rows):
    x = x_tile[...]
    inv = lax.rsqrt(jnp.mean(x * x, axis=-1, keepdims=True) + EPS)
    col_gain = g * (1.0 + mod_row[:, D_MODEL:2 * D_MODEL])
    acc = None
    for lo in range(0, D_MODEL, NORM_COLS):
        cols = slice(lo, lo + NORM_COLS)
        hk = (x_tile[:, cols] * inv * col_gain[:, cols] + mod_row[:, cols]).astype(jnp.bfloat16)
        h_ref[:, cols] = hk
        part = jnp.dot(hk, weight_rows(cols), preferred_element_type=jnp.float32)
        acc = part if acc is None else acc + part
    return acc


def _mlp_kernel(last, x_hbm, mod_ref, g_ref, w1_ref, w2_ref, *refs):
    if last:
        fg_ref, o_ref, h_ref, x_ref, sems = refs
    else:
        wa_ref, ba_ref, cond_ref, o_ref, next_mod_ref, h_ref, x_ref, sems = refs
    i = pl.program_id(0)
    j = pl.program_id(1)

    def finish(pre, base_ref):
        a = jnp.maximum(pre, 0.0)
        a = (a * a).astype(jnp.bfloat16)
        w2 = w2_ref[...].astype(jnp.bfloat16)
        for n in range(0, D_MODEL, MLP_TN):
            cols = slice(n, n + MLP_TN)
            gate = mod_ref[0, :, 2 * D_MODEL + n:2 * D_MODEL + n + MLP_TN]
            o_ref[:, cols] = base_ref[:, cols] + gate * jnp.dot(a, w2[:, cols], preferred_element_type=jnp.float32)
        if not last:
            next_mod_ref[...] = _ada_columns(wa_ref[...], cond_ref, ba_ref[...])

    @pl.when(j == 0)
    def _():
        _request_tile(x_hbm, x_ref, sems, i, i == 0)
        _request_tile(x_hbm, x_ref, sems, i + 1, i + 1 < pl.num_programs(0))
        x_tile = _await_tile(x_hbm, x_ref, sems, i)
        pre = _normalise_into_first_product(x_tile, g_ref[0], mod_ref[0], h_ref,
                                            lambda rows: w1_ref[rows, :].astype(jnp.bfloat16))
        finish(pre, x_tile)

    @pl.when(j > 0)
    def _():
        finish(jnp.dot(h_ref[...], w1_ref[...].astype(jnp.bfloat16), preferred_element_type=jnp.float32), o_ref)

    if last:
        @pl.when(j == pl.num_programs(1) - 1)
        def _():
            y = o_ref[...]
            inv = lax.rsqrt(jnp.mean(y * y, axis=-1, keepdims=True) + EPS)
            o_ref[...] = y * inv * fg_ref[...]


def _mlp_layer(x2d, mod, mod_idx, seq, norm_g, norm_idx, w1, w2, layer, final_g=None, ada=None):
    n_tok = x2d.shape[0]
    tiles_per_seq = seq // MLP_TM
    steps = D_FF // MLP_TF
    last = ada is None
    in_specs = [
        pl.BlockSpec(memory_space=pl.ANY),
        pl.BlockSpec((1, 1, 3 * D_MODEL), lambda i, j: (mod_idx + i // tiles_per_seq, 0, 0)),
        pl.BlockSpec((1, 1, D_MODEL), lambda i, j: (norm_idx, 0, 0)),
        pl.BlockSpec((None, D_MODEL, MLP_TF), lambda i, j: (layer, 0, j)),
        pl.BlockSpec((None, MLP_TF, D_MODEL), lambda i, j: (layer, j, 0)),
    ]
    x_spec = pl.BlockSpec((MLP_TM, D_MODEL), lambda i, j: (i, 0))
    x_shape = jax.ShapeDtypeStruct(x2d.shape, jnp.float32)
    if last:
        operands = (x2d, mod, norm_g, w1, w2, final_g)
        in_specs.append(pl.BlockSpec((1, D_MODEL), lambda i, j: (0, 0)))
        out_specs, out_shape = x_spec, x_shape
    else:
        ada_w, ada_b, cond = operands_ada = ada
        batch = cond.shape[0]
        blocks_per_row = 3 * D_MODEL // ADA_FUSED_TK
        n_blocks = 2 * blocks_per_row
        assert n_blocks <= (n_tok // MLP_TM) * steps

        def ada_block(i, j):
            blk = jnp.minimum(i * steps + j, n_blocks - 1)
            return 2 * (layer + 1) + blk // blocks_per_row, 0, blk % blocks_per_row

        def next_mod_block(i, j):
            row, _, col = ada_block(i, j)
            return row - 2 * (layer + 1), 0, col

        operands = (x2d, mod, norm_g, w1, w2) + operands_ada
        in_specs += [
            pl.BlockSpec((None, D_MODEL, ADA_FUSED_TK), ada_block),
            pl.BlockSpec((None, 1, ADA_FUSED_TK), ada_block),
            pl.BlockSpec(cond.shape, lambda i, j: (0, 0, 0)),
        ]
        out_specs = (x_spec, pl.BlockSpec((None, batch, ADA_FUSED_TK), next_mod_block))
        out_shape = (x_shape, jax.ShapeDtypeStruct((2, batch, 3 * D_MODEL), jnp.float32))
    return pl.pallas_call(
        functools.partial(_mlp_kernel, last),
        out_shape=out_shape,
        grid=(n_tok // MLP_TM, steps),
        in_specs=in_specs,
        out_specs=out_specs,
        scratch_shapes=[pltpu.VMEM((MLP_TM, D_MODEL), jnp.bfloat16),
                        pltpu.VMEM((2, MLP_TM, D_MODEL), jnp.float32),
                        pltpu.SemaphoreType.DMA((2, MLP_TM // ROW_CHUNK))],
        compiler_params=_params("arbitrary", "arbitrary"),
        name="mlp_final" if last else "mlp",
    )(*operands)


def _proj_kernel(x_hbm, mod_ref, g_ref, wqk_ref, wvg_ref, cos_ref, sin_ref, oqk_ref, ovg_ref, h_ref, x_ref, sems):
    i = pl.program_id(0)
    j = pl.program_id(1)
    n_q = RET_QK_DIM // PROJ_QK_TN

    def weights(rows=slice(None)):
        return jnp.concatenate([wqk_ref[rows, :].astype(jnp.bfloat16), wvg_ref[rows, :].astype(jnp.bfloat16)], axis=1)

    def finish(r):
        k_scale = jnp.where(j >= n_q, RET_HEAD_QK ** -0.5, 1.0)
        cos = cos_ref[...] * k_scale
        sin = sin_ref[...] * k_scale
        swap = lax.broadcasted_iota(jnp.int32, (PROJ_TM, LANES), 1) ^ 1
        for lo in range(0, PROJ_QK_TN, LANES):
            t = lo % RET_HEAD_QK
            xs = r[:, lo:lo + LANES]
            partner = jnp.take_along_axis(xs, swap, axis=1)
            oqk_ref[lo // PROJ_GROUP, :, t:t + LANES] = (
                xs * cos[:, t:t + LANES] + partner * sin[:, t:t + LANES]).astype(oqk_ref.dtype)
        for grp in range(PROJ_VG_TN // PROJ_GROUP):
            lo = PROJ_QK_TN + grp * PROJ_GROUP
            ovg_ref[grp] = r[:, lo:lo + PROJ_GROUP].astype(ovg_ref.dtype)

    @pl.when(j == 0)
    def _():
        _request_tile(x_hbm, x_ref, sems, i, i == 0)
        x_tile = _await_tile(x_hbm, x_ref, sems, i)
        finish(_normalise_into_first_product(x_tile, g_ref[0], mod_ref[0], h_ref, weights))

    _request_tile(x_hbm, x_ref, sems, i + 1, (j == 1) & (i + 1 < pl.num_programs(0)))

    @pl.when(j > 0)
    def _():
        finish(jnp.dot(h_ref[...], weights(), preferred_element_type=jnp.float32))


def _ret_projection(x2d, mod, mod_idx, seq, norm_g, norm_idx, w_in, layer, cos, sin):
    n_tok = x2d.shape[0]
    tiles_per_seq = seq // PROJ_TM
    steps = 2 * RET_QK_DIM // PROJ_QK_TN
    assert steps == 2 * RET_V_DIM // PROJ_VG_TN
    vg_off = 2 * RET_QK_DIM // PROJ_VG_TN
    return pl.pallas_call(
        _proj_kernel,
        out_shape=(jax.ShapeDtypeStruct((2 * RET_QK_DIM // PROJ_GROUP, n_tok, PROJ_GROUP), jnp.bfloat16),
                   jax.ShapeDtypeStruct((2 * RET_V_DIM // PROJ_GROUP, n_tok, PROJ_GROUP), jnp.bfloat16)),
        grid=(n_tok // PROJ_TM, steps),
        in_specs=[
            pl.BlockSpec(memory_space=pl.ANY),
            pl.BlockSpec((1, 1, 3 * D_MODEL), lambda i, j: (mod_idx + i // tiles_per_seq, 0, 0)),
            pl.BlockSpec((1, 1, D_MODEL), lambda i, j: (norm_idx, 0, 0)),
            pl.BlockSpec((None, D_MODEL, PROJ_QK_TN), lambda i, j: (layer, 0, j)),
            pl.BlockSpec((None, D_MODEL, PROJ_VG_TN), lambda i, j: (layer, 0, vg_off + j)),
            pl.BlockSpec((PROJ_TM, RET_HEAD_QK), lambda i, j: (i % tiles_per_seq, 0)),
            pl.BlockSpec((PROJ_TM, RET_HEAD_QK), lambda i, j: (i % tiles_per_seq, 0)),
        ],
        out_specs=(pl.BlockSpec((PROJ_QK_TN // PROJ_GROUP, PROJ_TM, PROJ_GROUP), lambda i, j: (j, i, 0)),
                   pl.BlockSpec((PROJ_VG_TN // PROJ_GROUP, PROJ_TM, PROJ_GROUP), lambda i, j: (j, i, 0))),
        scratch_shapes=[pltpu.VMEM((PROJ_TM, D_MODEL), jnp.bfloat16),
                        pltpu.VMEM((1, PROJ_TM, D_MODEL), jnp.float32),
                        pltpu.SemaphoreType.DMA((1, PROJ_TM // ROW_CHUNK))],
        compiler_params=_params("arbitrary", "arbitrary"),
        name="ret_projection",
    )(x2d, mod, norm_g, w_in, w_in, cos, sin)


def _scan_kernel(q_ref, k_ref, v_ref, g_ref, mask_ref, eps_ref, kdec_ref, cdec_ref, o_ref, state_ref):
    @pl.when(pl.program_id(2) == 0)
    def _():
        state_ref[...] = jnp.zeros_like(state_ref)

    mask = mask_ref[...]
    row_eps = eps_ref[...]
    k_dec = kdec_ref[...]
    chunk_dec = cdec_ref[...]
    state = state_ref[...]
    for c in range(q_ref.shape[0] // RET_CHUNK):
        rows = slice(c * RET_CHUNK, (c + 1) * RET_CHUNK)
        qc = q_ref[rows, :]
        kc = k_ref[rows, :]
        vc = jnp.concatenate([v_ref[grp, rows, :] for grp in range(v_ref.shape[0])], axis=1)
        gc = jnp.concatenate([g_ref[grp, rows, :] for grp in range(g_ref.shape[0])], axis=1)
        scores = lax.dot_general(qc, kc, (((1,), (1,)), ((), ())), preferred_element_type=jnp.float32)
        scores = (scores * mask).astype(jnp.bfloat16)
        y = (jnp.dot(scores, vc, preferred_element_type=jnp.float32)
             + jnp.dot(qc, state.astype(jnp.bfloat16), preferred_element_type=jnp.float32))
        k_decayed = (kc.astype(jnp.float32) * k_dec).astype(jnp.bfloat16)
        state = chunk_dec * state + lax.dot_general(
            k_decayed, vc, (((0,), (0,)), ((), ())), preferred_element_type=jnp.float32)
        mu = jnp.mean(y, axis=-1, keepdims=True)
        yc = y - mu
        var = jnp.mean(yc * yc, axis=-1, keepdims=True)
        yn = yc * lax.rsqrt(var + row_eps)
        z = (jax.nn.silu(gc.astype(jnp.float32)) * yn).astype(o_ref.dtype)
        for grp in range(o_ref.shape[0]):
            o_ref[grp, rows, :] = z[:, grp * PROJ_GROUP:(grp + 1) * PROJ_GROUP]
    state_ref[...] = state


def _ret_scan(proj_qk, proj_vg, batch, seq, mask, row_eps, k_dec, chunk_dec):
    assert RET_HEAD_QK == PROJ_GROUP
    n_tok = proj_qk.shape[1]
    steps = seq // RET_TS
    v_groups = RET_HEAD_V // PROJ_GROUP
    k_off = RET_HEADS
    v_off = 0
    g_off = RET_HEADS
    return pl.pallas_call(
        _scan_kernel,
        out_shape=jax.ShapeDtypeStruct((RET_V_DIM // PROJ_GROUP, n_tok, PROJ_GROUP), jnp.bfloat16),
        grid=(batch, RET_HEADS, steps),
        in_specs=[
            pl.BlockSpec((None, RET_TS, PROJ_GROUP), lambda b, h, t: (h, b * steps + t, 0)),
            pl.BlockSpec((None, RET_TS, PROJ_GROUP), lambda b, h, t: (k_off + h, b * steps + t, 0)),
            pl.BlockSpec((v_groups, RET_TS, PROJ_GROUP), lambda b, h, t: (v_off + h, b * steps + t, 0)),
            pl.BlockSpec((v_groups, RET_TS, PROJ_GROUP), lambda b, h, t: (g_off + h, b * steps + t, 0)),
            pl.BlockSpec((None, RET_CHUNK, RET_CHUNK), lambda b, h, t: (h, 0, 0)),
            pl.BlockSpec((None, RET_CHUNK, 1), lambda b, h, t: (h, 0, 0)),
            pl.BlockSpec((None, RET_CHUNK, 1), lambda b, h, t: (h, 0, 0)),
            pl.BlockSpec((None, 1, 1), lambda b, h, t: (h, 0, 0)),
        ],
        out_specs=pl.BlockSpec((v_groups, RET_TS, PROJ_GROUP), lambda b, h, t: (h, b * steps + t, 0)),
        scratch_shapes=[pltpu.VMEM((RET_HEAD_QK, RET_HEAD_V), jnp.float32)],
        compiler_params=_params("parallel", "parallel", "arbitrary"),
        name="ret_scan",
    )(proj_qk, proj_qk, proj_vg, proj_vg, mask, row_eps, k_dec, chunk_dec)


def _out_kernel(z_ref, w_ref, x_ref, gate_ref, o_ref, wb_ref):
    @pl.when(pl.program_id(1) == 0)
    def _():
        wb_ref[...] = w_ref[...].astype(jnp.bfloat16)

    y = None
    for grp in range(z_ref.shape[0]):
        part = jnp.dot(z_ref[grp], wb_ref[grp * PROJ_GROUP:(grp + 1) * PROJ_GROUP, :],
                       preferred_element_type=jnp.float32)
        y = part if y is None else y + part
    o_ref[...] = x_ref[...] + gate_ref[0] * y


def _ret_output(z, x2d, mod, mod_idx, seq, w_out, layer):
    n_tok = x2d.shape[0]
    tiles_per_seq = seq // OUT_TM
    gate_off = 2 * D_MODEL // OUT_TN
    return pl.pallas_call(
        _out_kernel,
        out_shape=jax.ShapeDtypeStruct(x2d.shape, jnp.float32),
        grid=(D_MODEL // OUT_TN, n_tok // OUT_TM),
        in_specs=[
            pl.BlockSpec((RET_V_DIM // PROJ_GROUP, OUT_TM, PROJ_GROUP), lambda j, i: (0, i, 0)),
            pl.BlockSpec((None, RET_V_DIM, OUT_TN), lambda j, i: (layer, 0, j)),
            pl.BlockSpec((OUT_TM, OUT_TN), lambda j, i: (i, j)),
            pl.BlockSpec((1, 1, OUT_TN), lambda j, i: (mod_idx + i // tiles_per_seq, 0, gate_off + j)),
        ],
        out_specs=pl.BlockSpec((OUT_TM, OUT_TN), lambda j, i: (i, j)),
        scratch_shapes=[pltpu.VMEM((RET_V_DIM, OUT_TN), jnp.bfloat16)],
        compiler_params=_params("parallel", "arbitrary"),
        name="ret_output",
    )(z, w_out, x2d, mod)


def _rotary_tables(seq):
    inv = ROPE_BASE ** (-np.arange(0, RET_HEAD_QK, 2, dtype=np.float64) / RET_HEAD_QK)
    ang = np.arange(seq, dtype=np.float64)[:, None] * inv[None, :]
    cos = np.repeat(np.cos(ang), 2, axis=1)
    sin = np.stack([-np.sin(ang), np.sin(ang)], axis=-1).reshape(seq, RET_HEAD_QK)
    return cos.astype(np.float32), sin.astype(np.float32)


def _decay_tables():
    gamma = 1.0 - 2.0 ** (-5.0 - np.arange(RET_HEADS, dtype=np.float64))
    log_g = np.log(gamma)
    j = np.arange(RET_CHUNK, dtype=np.float64)
    causal = j[:, None] >= j[None, :]
    mask = np.where(causal[None], np.exp(-(j[None, None, :] + 1.0) * log_g[:, None, None]), 0.0)
    row_eps = (EPS * np.exp(-2.0 * (j[None, :] + 1.0) * log_g[:, None]))[:, :, None]
    k_dec = np.exp((RET_CHUNK - 1.0 - j[None, :]) * log_g[:, None])[:, :, None]
    chunk_dec = np.exp(RET_CHUNK * log_g)[:, None, None]
    return tuple(t.astype(np.float32) for t in (mask, row_eps, k_dec, chunk_dec))


@jax.jit
def kernel(x, c, ada_w, ada_b, norm_g, pool_w, pool_scale, ret_w_in, ret_w_out, mlp_w1, mlp_w2, final_g):
    batch, seq, d = x.shape
    assert d == D_MODEL
    assert seq % max(POOL_TM, MLP_TM, PROJ_TM, OUT_TM, RET_TS) == 0

    ada_w = ada_w.reshape(DEPTH * 2, d, 3 * d)
    ada_b = ada_b.reshape(DEPTH * 2, 1, 3 * d)
    c_lanes = jnp.broadcast_to(c[:, :, None], (batch, d, LANES))
    mod, cond = _ada_first_layer(c_lanes, ada_w, ada_b)

    cos, sin = _rotary_tables(seq)
    mask, row_eps, k_dec, chunk_dec = _decay_tables()
    norm_rows = norm_g.reshape(DEPTH * 2, 1, d)
    scale_rows = pool_scale.reshape(-1, 1, d)

    x2d = x.reshape(batch * seq, d)
    for i in range(DEPTH):
        mod = mod.reshape(2 * batch, 1, 3 * d)
        mix_idx, mlp_idx = 0, batch
        if i % 2 == 0:
            x2d = _pool_layer(x2d, mod, mix_idx, seq, norm_rows, 2 * i, pool_w, scale_rows, i // 2)
        else:
            proj_qk, proj_vg = _ret_projection(x2d, mod, mix_idx, seq, norm_rows, 2 * i, ret_w_in, i // 2, cos, sin)
            z = _ret_scan(proj_qk, proj_vg, batch, seq, mask, row_eps, k_dec, chunk_dec)
            x2d = _ret_output(z, x2d, mod, mix_idx, seq, ret_w_out, i // 2)
        if i < DEPTH - 1:
            x2d, mod = _mlp_layer(x2d, mod, mlp_idx, seq, norm_rows, 2 * i + 1, mlp_w1, mlp_w2, i,
                                  ada=(ada_w, ada_b, cond))
        else:
            x2d = _mlp_layer(x2d, mod, mlp_idx, seq, norm_rows, 2 * i + 1, mlp_w1, mlp_w2, i,
                             final_g=final_g.reshape(1, d))
    return x2d.reshape(batch, seq, d)
```

```python
import functools

import jax
import jax.numpy as jnp
import numpy as np
from jax import lax
from jax.experimental import pallas as pl
from jax.experimental.pallas import tpu as pltpu

D_MODEL = 2048
DEPTH = 4
POOL_WINDOWS = (2, 4, 8, 16)
POOL_GROUP = D_MODEL // len(POOL_WINDOWS)
POOL_HALO = 16
RET_HEADS = 8
RET_QK_DIM = D_MODEL
RET_V_DIM = 2 * D_MODEL
RET_HEAD_QK = RET_QK_DIM // RET_HEADS
RET_HEAD_V = RET_V_DIM // RET_HEADS
RET_PROJ = 2 * RET_QK_DIM + 2 * RET_V_DIM
ROPE_BASE = 10000.0
D_FF = 4 * D_MODEL
EPS = 1e-6
LANES = 128

V7X_VMEM_BYTES = 64 * 1024 * 1024
VMEM_LIMIT_BYTES = V7X_VMEM_BYTES - 4 * 1024 * 1024

ROW_CHUNK = 128
NORM_COLS = 256
ADA_TK = 1024
ADA_FUSED_TK = 128
POOL_TM = 1024
MLP_TM = 1024
MLP_TF = 512
MLP_TN = 512
PROJ_TM = 1024
PROJ_QK_TN = 512
PROJ_VG_TN = 1024
PROJ_GROUP = 256
RET_TS = 4096
RET_CHUNK = 256
OUT_TM = 1024
OUT_TN = 512


def _params(*semantics):
    return pltpu.CompilerParams(dimension_semantics=semantics, vmem_limit_bytes=VMEM_LIMIT_BYTES)


def _norm_modulate(x, g, mod_row):
    shift = mod_row[:, 0:D_MODEL]
    scale = mod_row[:, D_MODEL:2 * D_MODEL]
    inv = lax.rsqrt(jnp.mean(x * x, axis=-1, keepdims=True) + EPS)
    return x * inv * (g * (1.0 + scale)) + shift


def _ada_columns(w, cond_ref, bias):
    rows = []
    for b in range(cond_ref.shape[0]):
        cond = cond_ref[b]
        rows.append(jnp.concatenate(
            [jnp.sum(w[:, lo:lo + LANES] * cond, axis=0, keepdims=True) for lo in range(0, w.shape[1], LANES)],
            axis=1))
    return jnp.concatenate(rows, axis=0) + bias


def _ada_kernel(c_ref, w_ref, b_ref, o_ref, cond_ref):
    cond_ref[...] = jax.nn.silu(c_ref[...])
    o_ref[...] = _ada_columns(w_ref[...], cond_ref, b_ref[...])


def _ada_first_layer(c_lanes, ada_w, ada_b):
    batch = c_lanes.shape[0]
    n_out = ada_w.shape[2]
    return pl.pallas_call(
        _ada_kernel,
        out_shape=(jax.ShapeDtypeStruct((2, batch, n_out), jnp.float32),
                   jax.ShapeDtypeStruct(c_lanes.shape, jnp.float32)),
        grid=(2, n_out // ADA_TK),
        in_specs=[
            pl.BlockSpec(c_lanes.shape, lambda l, j: (0, 0, 0)),
            pl.BlockSpec((None, D_MODEL, ADA_TK), lambda l, j: (l, 0, j)),
            pl.BlockSpec((None, 1, ADA_TK), lambda l, j: (l, 0, j)),
        ],
        out_specs=(pl.BlockSpec((None, batch, ADA_TK), lambda l, j: (l, 0, j)),
                   pl.BlockSpec(c_lanes.shape, lambda l, j: (0, 0, 0))),
        compiler_params=_params("arbitrary", "arbitrary"),
        name="ada_modulation",
    )(c_lanes, ada_w, ada_b)


def _pool_kernel(tiles_per_seq, x_ref, halo_ref, mod_ref, g_ref, w_ref, ls_ref, o_ref, wb_ref):
    i = pl.program_id(0)

    @pl.when(i == 0)
    def _():
        wb_ref[...] = w_ref[...].astype(jnp.bfloat16)

    tile_in_seq = i % tiles_per_seq
    tm = x_ref.shape[0]
    mod_row = mod_ref[0]
    g = g_ref[0]
    x = x_ref[...]
    h = _norm_modulate(x, g, mod_row)
    h_halo = _norm_modulate(halo_ref[...], g, mod_row)
    h_halo = jnp.where(tile_in_seq == 0, 0.0, h_halo)
    out_scale = mod_row[:, 2 * D_MODEL:3 * D_MODEL] * ls_ref[0]
    pos = tile_in_seq * tm + lax.broadcasted_iota(jnp.int32, (tm, 1), 0)
    for gi, window in enumerate(POOL_WINDOWS):
        cols = slice(gi * POOL_GROUP, (gi + 1) * POOL_GROUP)
        hg = h[:, cols]
        acc = jnp.concatenate([h_halo[:, cols], hg], axis=0)
        shift = 1
        while shift < window:
            acc = acc + pltpu.roll(acc, shift, axis=0)
            shift *= 2
        cnt = jnp.minimum(pos + 1, window).astype(jnp.float32)
        p = acc[POOL_HALO:, :] / cnt - hg
        y = jnp.dot(p.astype(jnp.bfloat16), wb_ref[gi], preferred_element_type=jnp.float32)
        o_ref[:, cols] = x[:, cols] + out_scale[:, cols] * y


def _pool_layer(x2d, mod, mod_idx, seq, norm_g, norm_idx, pool_w, pool_scale, pool_idx):
    n_tok = x2d.shape[0]
    tiles_per_seq = seq // POOL_TM
    halo_blocks_per_tile = POOL_TM // POOL_HALO
    n_grp = len(POOL_WINDOWS)
    return pl.pallas_call(
        functools.partial(_pool_kernel, tiles_per_seq),
        out_shape=jax.ShapeDtypeStruct(x2d.shape, jnp.float32),
        grid=(n_tok // POOL_TM,),
        in_specs=[
            pl.BlockSpec((POOL_TM, D_MODEL), lambda i: (i, 0)),
            pl.BlockSpec((POOL_HALO, D_MODEL), lambda i: (jnp.maximum(i * halo_blocks_per_tile - 1, 0), 0)),
            pl.BlockSpec((1, 1, 3 * D_MODEL), lambda i: (mod_idx + i // tiles_per_seq, 0, 0)),
            pl.BlockSpec((1, 1, D_MODEL), lambda i: (norm_idx, 0, 0)),
            pl.BlockSpec((None, n_grp, POOL_GROUP, POOL_GROUP), lambda i: (pool_idx, 0, 0, 0)),
            pl.BlockSpec((1, 1, D_MODEL), lambda i: (pool_idx, 0, 0)),
        ],
        out_specs=pl.BlockSpec((POOL_TM, D_MODEL), lambda i: (i, 0)),
        scratch_shapes=[pltpu.VMEM((n_grp, POOL_GROUP, POOL_GROUP), jnp.bfloat16)],
        compiler_params=_params("arbitrary"),
        name="pool_mixer",
    )(x2d, x2d, mod, norm_g, pool_w, pool_scale)


def _tile_copies(x_hbm, x_ref, sems, tile):
    slots, tm, _ = x_ref.shape
    slot = tile % slots
    first_row = pl.multiple_of(tile * tm, tm)
    return [
        pltpu.make_async_copy(x_hbm.at[pl.ds(first_row + k * ROW_CHUNK, ROW_CHUNK), :],
                              x_ref.at[slot, pl.ds(k * ROW_CHUNK, ROW_CHUNK), :], sems.at[slot, k])
        for k in range(tm // ROW_CHUNK)
    ]


def _request_tile(x_hbm, x_ref, sems, tile, when):
    @pl.when(when)
    def _():
        for cp in _tile_copies(x_hbm, x_ref, sems, tile):
            cp.start()


def _await_tile(x_hbm, x_ref, sems, tile):
    for cp in _tile_copies(x_hbm, x_ref, sems, tile):
        cp.wait()
    return x_ref.at[tile % x_ref.shape[0]]


def _normalise_into_first_product(x_tile, g, mod_row, h_ref, weight_rows):
    x = x_tile[...]
    inv = lax.rsqrt(jnp.mean(x * x, axis=-1, keepdims=True) + EPS)
    col_gain = g * (1.0 + mod_row[:, D_MODEL:2 * D_MODEL])
    acc = None
    for lo in range(0, D_MODEL, NORM_COLS):
        cols = slice(lo, lo + NORM_COLS)
        hk = (x_tile[:, cols] * inv * col_gain[:, cols] + mod_row[:, cols]).astype(jnp.bfloat16)
        h_ref[:, cols] = hk
        part = jnp.dot(hk, weight_rows(cols), preferred_element_type=jnp.float32)
        acc = part if acc is None else acc + part
    return acc


def _mlp_kernel(last, x_hbm, mod_ref, g_ref, w1_ref, w2_ref, *refs):
    if last:
        fg_ref, o_ref, h_ref, x_ref, sems = refs
    else:
        wa_ref, ba_ref, cond_ref, o_ref, next_mod_ref, h_ref, x_ref, sems = refs
    i = pl.program_id(0)
    j = pl.program_id(1)

    def finish(pre, base_ref):
        a = jnp.maximum(pre, 0.0)
        a = (a * a).astype(jnp.bfloat16)
        w2 = w2_ref[...].astype(jnp.bfloat16)
        for n in range(0, D_MODEL, MLP_TN):
            cols = slice(n, n + MLP_TN)
            gate = mod_ref[0, :, 2 * D_MODEL + n:2 * D_MODEL + n + MLP_TN]
            o_ref[:, cols] = base_ref[:, cols] + gate * jnp.dot(a, w2[:, cols], preferred_element_type=jnp.float32)
        if not last:
            next_mod_ref[...] = _ada_columns(wa_ref[...], cond_ref, ba_ref[...])

    @pl.when(j == 0)
    def _():
        _request_tile(x_hbm, x_ref, sems, i, i == 0)
        _request_tile(x_hbm, x_ref, sems, i + 1, i + 1 < pl.num_programs(0))
        x_tile = _await_tile(x_hbm, x_ref, sems, i)
        pre = _normalise_into_first_product(x_tile, g_ref[0], mod_ref[0], h_ref,
                                            lambda rows: w1_ref[rows, :].astype(jnp.bfloat16))
        finish(pre, x_tile)

    @pl.when(j > 0)
    def _():
        finish(jnp.dot(h_ref[...], w1_ref[...].astype(jnp.bfloat16), preferred_element_type=jnp.float32), o_ref)

    if last:
        @pl.when(j == pl.num_programs(1) - 1)
        def _():
            y = o_ref[...]
            inv = lax.rsqrt(jnp.mean(y * y, axis=-1, keepdims=True) + EPS)
            o_ref[...] = y * inv * fg_ref[...]


def _mlp_layer(x2d, mod, mod_idx, seq, norm_g, norm_idx, w1, w2, layer, final_g=None, ada=None):
    n_tok = x2d.shape[0]
    tiles_per_seq = seq // MLP_TM
    steps = D_FF // MLP_TF
    last = ada is None
    in_specs = [
        pl.BlockSpec(memory_space=pl.ANY),
        pl.BlockSpec((1, 1, 3 * D_MODEL), lambda i, j: (mod_idx + i // tiles_per_seq, 0, 0)),
        pl.BlockSpec((1, 1, D_MODEL), lambda i, j: (norm_idx, 0, 0)),
        pl.BlockSpec((None, D_MODEL, MLP_TF), lambda i, j: (layer, 0, j)),
        pl.BlockSpec((None, MLP_TF, D_MODEL), lambda i, j: (layer, j, 0)),
    ]
    x_spec = pl.BlockSpec((MLP_TM, D_MODEL), lambda i, j: (i, 0))
    x_shape = jax.ShapeDtypeStruct(x2d.shape, jnp.float32)
    if last:
        operands = (x2d, mod, norm_g, w1, w2, final_g)
        in_specs.append(pl.BlockSpec((1, D_MODEL), lambda i, j: (0, 0)))
        out_specs, out_shape = x_spec, x_shape
    else:
        ada_w, ada_b, cond = operands_ada = ada
        batch = cond.shape[0]
        blocks_per_row = 3 * D_MODEL // ADA_FUSED_TK
        n_blocks = 2 * blocks_per_row
        assert n_blocks <= (n_tok // MLP_TM) * steps

        def ada_block(i, j):
            blk = jnp.minimum(i * steps + j, n_blocks - 1)
            return 2 * (layer + 1) + blk // blocks_per_row, 0, blk % blocks_per_row

        def next_mod_block(i, j):
            row, _, col = ada_block(i, j)
            return row - 2 * (layer + 1), 0, col

        operands = (x2d, mod, norm_g, w1, w2) + operands_ada
        in_specs += [
            pl.BlockSpec((None, D_MODEL, ADA_FUSED_TK), ada_block),
            pl.BlockSpec((None, 1, ADA_FUSED_TK), ada_block),
            pl.BlockSpec(cond.shape, lambda i, j: (0, 0, 0)),
        ]
        out_specs = (x_spec, pl.BlockSpec((None, batch, ADA_FUSED_TK), next_mod_block))
        out_shape = (x_shape, jax.ShapeDtypeStruct((2, batch, 3 * D_MODEL), jnp.float32))
    return pl.pallas_call(
        functools.partial(_mlp_kernel, last),
        out_shape=out_shape,
        grid=(n_tok // MLP_TM, steps),
        in_specs=in_specs,
        out_specs=out_specs,
        scratch_shapes=[pltpu.VMEM((MLP_TM, D_MODEL), jnp.bfloat16),
                        pltpu.VMEM((2, MLP_TM, D_MODEL), jnp.float32),
                        pltpu.SemaphoreType.DMA((2, MLP_TM // ROW_CHUNK))],
        compiler_params=_params("arbitrary", "arbitrary"),
        name="mlp_final" if last else "mlp",
    )(*operands)


def _proj_kernel(x_hbm, mod_ref, g_ref, wqk_ref, wvg_ref, cos_ref, sin_ref, oqk_ref, ovg_ref, h_ref, x_ref, sems):
    i = pl.program_id(0)
    j = pl.program_id(1)
    n_q = RET_QK_DIM // PROJ_QK_TN

    def weights(rows=slice(None)):
        return jnp.concatenate([wqk_ref[rows, :].astype(jnp.bfloat16), wvg_ref[rows, :].astype(jnp.bfloat16)], axis=1)

    def finish(r):
        k_scale = jnp.where(j >= n_q, RET_HEAD_QK ** -0.5, 1.0)
        cos = cos_ref[...] * k_scale
        sin = sin_ref[...] * k_scale
        swap = lax.broadcasted_iota(jnp.int32, (PROJ_TM, LANES), 1) ^ 1
        for lo in range(0, PROJ_QK_TN, LANES):
            t = lo % RET_HEAD_QK
            xs = r[:, lo:lo + LANES]
            partner = jnp.take_along_axis(xs, swap, axis=1)
            oqk_ref[lo // PROJ_GROUP, :, t:t + LANES] = (
                xs * cos[:, t:t + LANES] + partner * sin[:, t:t + LANES]).astype(oqk_ref.dtype)
        for grp in range(PROJ_VG_TN // PROJ_GROUP):
            lo = PROJ_QK_TN + grp * PROJ_GROUP
            ovg_ref[grp] = r[:, lo:lo + PROJ_GROUP].astype(ovg_ref.dtype)

    @pl.when(j == 0)
    def _():
        _request_tile(x_hbm, x_ref, sems, i, i == 0)
        x_tile = _await_tile(x_hbm, x_ref, sems, i)
        finish(_normalise_into_first_product(x_tile, g_ref[0], mod_ref[0], h_ref, weights))

    _request_tile(x_hbm, x_ref, sems, i + 1, (j == 1) & (i + 1 < pl.num_programs(0)))

    @pl.when(j > 0)
    def _():
        finish(jnp.dot(h_ref[...], weights(), preferred_element_type=jnp.float32))


def _ret_projection(x2d, mod, mod_idx, seq, norm_g, norm_idx, w_in, layer, cos, sin):
    n_tok = x2d.shape[0]
    tiles_per_seq = seq // PROJ_TM
    steps = 2 * RET_QK_DIM // PROJ_QK_TN
    assert steps == 2 * RET_V_DIM // PROJ_VG_TN
    vg_off = 2 * RET_QK_DIM // PROJ_VG_TN
    return pl.pallas_call(
        _proj_kernel,
        out_shape=(jax.ShapeDtypeStruct((2 * RET_QK_DIM // PROJ_GROUP, n_tok, PROJ_GROUP), jnp.bfloat16),
                   jax.ShapeDtypeStruct((2 * RET_V_DIM // PROJ_GROUP, n_tok, PROJ_GROUP), jnp.bfloat16)),
        grid=(n_tok // PROJ_TM, steps),
        in_specs=[
            pl.BlockSpec(memory_space=pl.ANY),
            pl.BlockSpec((1, 1, 3 * D_MODEL), lambda i, j: (mod_idx + i // tiles_per_seq, 0, 0)),
            pl.BlockSpec((1, 1, D_MODEL), lambda i, j: (norm_idx, 0, 0)),
            pl.BlockSpec((None, D_MODEL, PROJ_QK_TN), lambda i, j: (layer, 0, j)),
            pl.BlockSpec((None, D_MODEL, PROJ_VG_TN), lambda i, j: (layer, 0, vg_off + j)),
            pl.BlockSpec((PROJ_TM, RET_HEAD_QK), lambda i, j: (i % tiles_per_seq, 0)),
            pl.BlockSpec((PROJ_TM, RET_HEAD_QK), lambda i, j: (i % tiles_per_seq, 0)),
        ],
        out_specs=(pl.BlockSpec((PROJ_QK_TN // PROJ_GROUP, PROJ_TM, PROJ_GROUP), lambda i, j: (j, i, 0)),
                   pl.BlockSpec((PROJ_VG_TN // PROJ_GROUP, PROJ_TM, PROJ_GROUP), lambda i, j: (j, i, 0))),
        scratch_shapes=[pltpu.VMEM((PROJ_TM, D_MODEL), jnp.bfloat16),
                        pltpu.VMEM((1, PROJ_TM, D_MODEL), jnp.float32),
                        pltpu.SemaphoreType.DMA((1, PROJ_TM // ROW_CHUNK))],
        compiler_params=_params("arbitrary", "arbitrary"),
        name="ret_projection",
    )(x2d, mod, norm_g, w_in, w_in, cos, sin)


def _scan_kernel(q_ref, k_ref, v_ref, g_ref, mask_ref, eps_ref, kdec_ref, cdec_ref, o_ref, state_ref):
    @pl.when(pl.program_id(2) == 0)
    def _():
        state_ref[...] = jnp.zeros_like(state_ref)

    mask = mask_ref[...]
    row_eps = eps_ref[...]
    k_dec = kdec_ref[...]
    chunk_dec = cdec_ref[...]
    state = state_ref[...]
    for c in range(q_ref.shape[0] // RET_CHUNK):
        rows = slice(c * RET_CHUNK, (c + 1) * RET_CHUNK)
        qc = q_ref[rows, :]
        kc = k_ref[rows, :]
        vc = jnp.concatenate([v_ref[grp, rows, :] for grp in range(v_ref.shape[0])], axis=1)
        gc = jnp.concatenate([g_ref[grp, rows, :] for grp in range(g_ref.shape[0])], axis=1)
        scores = lax.dot_general(qc, kc, (((1,), (1,)), ((), ())), preferred_element_type=jnp.float32)
        scores = (scores * mask).astype(jnp.bfloat16)
        y = (jnp.dot(scores, vc, preferred_element_type=jnp.float32)
             + jnp.dot(qc, state.astype(jnp.bfloat16), preferred_element_type=jnp.float32))
        k_decayed = (kc.astype(jnp.float32) * k_dec).astype(jnp.bfloat16)
        state = chunk_dec * state + lax.dot_general(
            k_decayed, vc, (((0,), (0,)), ((), ())), preferred_element_type=jnp.float32)
        mu = jnp.mean(y, axis=-1, keepdims=True)
        yc = y - mu
        var = jnp.mean(yc * yc, axis=-1, keepdims=True)
        yn = yc * lax.rsqrt(var + row_eps)
        z = (jax.nn.silu(gc.astype(jnp.float32)) * yn).astype(o_ref.dtype)
        for grp in range(o_ref.shape[0]):
            o_ref[grp, rows, :] = z[:, grp * PROJ_GROUP:(grp + 1) * PROJ_GROUP]
    state_ref[...] = state


def _ret_scan(proj_qk, proj_vg, batch, seq, mask, row_eps, k_dec, chunk_dec):
    assert RET_HEAD_QK == PROJ_GROUP
    n_tok = proj_qk.shape[1]
    steps = seq // RET_TS
    v_groups = RET_HEAD_V // PROJ_GROUP
    k_off = RET_HEADS
    v_off = 0
    g_off = RET_HEADS
    return pl.pallas_call(
        _scan_kernel,
        out_shape=jax.ShapeDtypeStruct((RET_V_DIM // PROJ_GROUP, n_tok, PROJ_GROUP), jnp.bfloat16),
        grid=(batch, RET_HEADS, steps),
        in_specs=[
            pl.BlockSpec((None, RET_TS, PROJ_GROUP), lambda b, h, t: (h, b * steps + t, 0)),
            pl.BlockSpec((None, RET_TS, PROJ_GROUP), lambda b, h, t: (k_off + h, b * steps + t, 0)),
            pl.BlockSpec((v_groups, RET_TS, PROJ_GROUP), lambda b, h, t: (v_off + h, b * steps + t, 0)),
            pl.BlockSpec((v_groups, RET_TS, PROJ_GROUP), lambda b, h, t: (g_off + h, b * steps + t, 0)),
            pl.BlockSpec((None, RET_CHUNK, RET_CHUNK), lambda b, h, t: (h, 0, 0)),
            pl.BlockSpec((None, RET_CHUNK, 1), lambda b, h, t: (h, 0, 0)),
            pl.BlockSpec((None, RET_CHUNK, 1), lambda b, h, t: (h, 0, 0)),
            pl.BlockSpec((None, 1, 1), lambda b, h, t: (h, 0, 0)),
        ],
        out_specs=pl.BlockSpec((v_groups, RET_TS, PROJ_GROUP), lambda b, h, t: (h, b * steps + t, 0)),
        scratch_shapes=[pltpu.VMEM((RET_HEAD_QK, RET_HEAD_V), jnp.float32)],
        compiler_params=_params("parallel", "parallel", "arbitrary"),
        name="ret_scan",
    )(proj_qk, proj_qk, proj_vg, proj_vg, mask, row_eps, k_dec, chunk_dec)


def _out_kernel(z_ref, w_ref, x_ref, gate_ref, o_ref, wb_ref):
    j = pl.program_id(1)

    @pl.when(pl.program_id(0) == 0)
    def _():
        wb_ref[j] = w_ref[...].astype(jnp.bfloat16)

    y = None
    for grp in range(z_ref.shape[0]):
        part = jnp.dot(z_ref[grp], wb_ref[j, grp * PROJ_GROUP:(grp + 1) * PROJ_GROUP, :],
                       preferred_element_type=jnp.float32)
        y = part if y is None else y + part
    o_ref[...] = x_ref[...] + gate_ref[0] * y


def _ret_output(z, x2d, mod, mod_idx, seq, w_out, layer):
    n_tok = x2d.shape[0]
    tiles_per_seq = seq // OUT_TM
    n_col = D_MODEL // OUT_TN
    gate_off = 2 * D_MODEL // OUT_TN
    return pl.pallas_call(
        _out_kernel,
        out_shape=jax.ShapeDtypeStruct(x2d.shape, jnp.float32),
        grid=(n_tok // OUT_TM, n_col),
        in_specs=[
            pl.BlockSpec((RET_V_DIM // PROJ_GROUP, OUT_TM, PROJ_GROUP), lambda i, j: (0, i, 0)),
            pl.BlockSpec((None, RET_V_DIM, OUT_TN), lambda i, j: (layer, 0, jnp.where(i == 0, j, n_col - 1))),
            pl.BlockSpec((OUT_TM, OUT_TN), lambda i, j: (i, j)),
            pl.BlockSpec((1, 1, OUT_TN), lambda i, j: (mod_idx + i // tiles_per_seq, 0, gate_off + j)),
        ],
        out_specs=pl.BlockSpec((OUT_TM, OUT_TN), lambda i, j: (i, j)),
        scratch_shapes=[pltpu.VMEM((n_col, RET_V_DIM, OUT_TN), jnp.bfloat16)],
        compiler_params=_params("arbitrary", "arbitrary"),
        name="ret_output",
    )(z, w_out, x2d, mod)


def _rotary_tables(seq):
    inv = ROPE_BASE ** (-np.arange(0, RET_HEAD_QK, 2, dtype=np.float64) / RET_HEAD_QK)
    ang = np.arange(seq, dtype=np.float64)[:, None] * inv[None, :]
    cos = np.repeat(np.cos(ang), 2, axis=1)
    sin = np.stack([-np.sin(ang), np.sin(ang)], axis=-1).reshape(seq, RET_HEAD_QK)
    return cos.astype(np.float32), sin.astype(np.float32)


def _decay_tables():
    gamma = 1.0 - 2.0 ** (-5.0 - np.arange(RET_HEADS, dtype=np.float64))
    log_g = np.log(gamma)
    j = np.arange(RET_CHUNK, dtype=np.float64)
    causal = j[:, None] >= j[None, :]
    mask = np.where(causal[None], np.exp(-(j[None, None, :] + 1.0) * log_g[:, None, None]), 0.0)
    row_eps = (EPS * np.exp(-2.0 * (j[None, :] + 1.0) * log_g[:, None]))[:, :, None]
    k_dec = np.exp((RET_CHUNK - 1.0 - j[None, :]) * log_g[:, None])[:, :, None]
    chunk_dec = np.exp(RET_CHUNK * log_g)[:, None, None]
    return tuple(t.astype(np.float32) for t in (mask, row_eps, k_dec, chunk_dec))


@jax.jit
def kernel(x, c, ada_w, ada_b, norm_g, pool_w, pool_scale, ret_w_in, ret_w_out, mlp_w1, mlp_w2, final_g):
    batch, seq, d = x.shape
    assert d == D_MODEL
    assert seq % max(POOL_TM, MLP_TM, PROJ_TM, OUT_TM, RET_TS) == 0

    ada_w = ada_w.reshape(DEPTH * 2, d, 3 * d)
    ada_b = ada_b.reshape(DEPTH * 2, 1, 3 * d)
    c_lanes = jnp.broadcast_to(c[:, :, None], (batch, d, LANES))
    mod, cond = _ada_first_layer(c_lanes, ada_w, ada_b)

    cos, sin = _rotary_tables(seq)
    mask, row_eps, k_dec, chunk_dec = _decay_tables()
    norm_rows = norm_g.reshape(DEPTH * 2, 1, d)
    scale_rows = pool_scale.reshape(-1, 1, d)

    x2d = x.reshape(batch * seq, d)
    for i in range(DEPTH):
        mod = mod.reshape(2 * batch, 1, 3 * d)
        mix_idx, mlp_idx = 0, batch
        if i % 2 == 0:
            x2d = _pool_layer(x2d, mod, mix_idx, seq, norm_rows, 2 * i, pool_w, scale_rows, i // 2)
        else:
            proj_qk, proj_vg = _ret_projection(x2d, mod, mix_idx, seq, norm_rows, 2 * i, ret_w_in, i // 2, cos, sin)
            z = _ret_scan(proj_qk, proj_vg, batch, seq, mask, row_eps, k_dec, chunk_dec)
            x2d = _ret_output(z, x2d, mod, mix_idx, seq, ret_w_out, i // 2)
        if i < DEPTH - 1:
            x2d, mod = _mlp_layer(x2d, mod, mlp_idx, seq, norm_rows, 2 * i + 1, mlp_w1, mlp_w2, i,
                                  ada=(ada_w, ada_b, cond))
        else:
            x2d = _mlp_layer(x2d, mod, mlp_idx, seq, norm_rows, 2 * i + 1, mlp_w1, mlp_w2, i,
                             final_g=final_g.reshape(1, d))
    return x2d.reshape(batch, seq, d)
```

```python
import functools

import jax
import jax.numpy as jnp
import numpy as np
from jax import lax
from jax.experimental import pallas as pl
from jax.experimental.pallas import tpu as pltpu

D_MODEL = 2048
DEPTH = 4
POOL_WINDOWS = (2, 4, 8, 16)
POOL_GROUP = D_MODEL // len(POOL_WINDOWS)
POOL_HALO = 16
RET_HEADS = 8
RET_QK_DIM = D_MODEL
RET_V_DIM = 2 * D_MODEL
RET_HEAD_QK = RET_QK_DIM // RET_HEADS
RET_HEAD_V = RET_V_DIM // RET_HEADS
RET_PROJ = 2 * RET_QK_DIM + 2 * RET_V_DIM
ROPE_BASE = 10000.0
D_FF = 4 * D_MODEL
EPS = 1e-6
LANES = 128

V7X_VMEM_BYTES = 64 * 1024 * 1024
VMEM_LIMIT_BYTES = V7X_VMEM_BYTES - 4 * 1024 * 1024

ROW_CHUNK = 128
NORM_COLS = 256
ADA_TK = 1024
ADA_FUSED_TK = 128
POOL_TM = 1024
MLP_TM = 1024
MLP_TF = 512
MLP_TN = 512
PROJ_TM = 1024
PROJ_QK_TN = 512
PROJ_VG_TN = 1024
PROJ_GROUP = 256
RET_TS = 4096
RET_CHUNK = 256
OUT_TM = 1024
OUT_TN = 512


def _params(*semantics):
    return pltpu.CompilerParams(dimension_semantics=semantics, vmem_limit_bytes=VMEM_LIMIT_BYTES)


def _norm_modulate(x, g, mod_row):
    shift = mod_row[:, 0:D_MODEL]
    scale = mod_row[:, D_MODEL:2 * D_MODEL]
    inv = lax.rsqrt(jnp.mean(x * x, axis=-1, keepdims=True) + EPS)
    return x * inv * (g * (1.0 + scale)) + shift


def _ada_columns(w, cond_ref, bias):
    rows = []
    for b in range(cond_ref.shape[0]):
        cond = cond_ref[b]
        rows.append(jnp.concatenate(
            [jnp.sum(w[:, lo:lo + LANES] * cond, axis=0, keepdims=True) for lo in range(0, w.shape[1], LANES)],
            axis=1))
    return jnp.concatenate(rows, axis=0) + bias


def _ada_kernel(c_ref, w_ref, b_ref, o_ref, cond_ref):
    cond_ref[...] = jax.nn.silu(c_ref[...])
    o_ref[...] = _ada_columns(w_ref[...], cond_ref, b_ref[...])


def _ada_first_layer(c_lanes, ada_w, ada_b):
    batch = c_lanes.shape[0]
    n_out = ada_w.shape[2]
    return pl.pallas_call(
        _ada_kernel,
        out_shape=(jax.ShapeDtypeStruct((2, batch, n_out), jnp.float32),
                   jax.ShapeDtypeStruct(c_lanes.shape, jnp.float32)),
        grid=(2, n_out // ADA_TK),
        in_specs=[
            pl.BlockSpec(c_lanes.shape, lambda l, j: (0, 0, 0)),
            pl.BlockSpec((None, D_MODEL, ADA_TK), lambda l, j: (l, 0, j)),
            pl.BlockSpec((None, 1, ADA_TK), lambda l, j: (l, 0, j)),
        ],
        out_specs=(pl.BlockSpec((None, batch, ADA_TK), lambda l, j: (l, 0, j)),
                   pl.BlockSpec(c_lanes.shape, lambda l, j: (0, 0, 0))),
        compiler_params=_params("arbitrary", "arbitrary"),
        name="ada_modulation",
    )(c_lanes, ada_w, ada_b)


def _pool_kernel(tiles_per_seq, x_ref, halo_ref, mod_ref, g_ref, w_ref, ls_ref, o_ref, wb_ref):
    i = pl.program_id(0)

    @pl.when(i == 0)
    def _():
        wb_ref[...] = w_ref[...].astype(jnp.bfloat16)

    tile_in_seq = i % tiles_per_seq
    tm = x_ref.shape[0]
    mod_row = mod_ref[0]
    g = g_ref[0]
    x = x_ref[...]
    h = _norm_modulate(x, g, mod_row)
    h_halo = _norm_modulate(halo_ref[...], g, mod_row)
    h_halo = jnp.where(tile_in_seq == 0, 0.0, h_halo)
    out_scale = mod_row[:, 2 * D_MODEL:3 * D_MODEL] * ls_ref[0]
    pos = tile_in_seq * tm + lax.broadcasted_iota(jnp.int32, (tm, 1), 0)
    for gi, window in enumerate(POOL_WINDOWS):
        cols = slice(gi * POOL_GROUP, (gi + 1) * POOL_GROUP)
        hg = h[:, cols]
        acc = jnp.concatenate([h_halo[:, cols], hg], axis=0)
        shift = 1
        while shift < window:
            acc = acc + pltpu.roll(acc, shift, axis=0)
            shift *= 2
        cnt = jnp.minimum(pos + 1, window).astype(jnp.float32)
        p = acc[POOL_HALO:, :] / cnt - hg
        y = jnp.dot(p.astype(jnp.bfloat16), wb_ref[gi], preferred_element_type=jnp.float32)
        o_ref[:, cols] = x[:, cols] + out_scale[:, cols] * y


def _pool_layer(x2d, mod, mod_idx, seq, norm_g, norm_idx, pool_w, pool_scale, pool_idx):
    n_tok = x2d.shape[0]
    tiles_per_seq = seq // POOL_TM
    halo_blocks_per_tile = POOL_TM // POOL_HALO
    n_grp = len(POOL_WINDOWS)
    return pl.pallas_call(
        functools.partial(_pool_kernel, tiles_per_seq),
        out_shape=jax.ShapeDtypeStruct(x2d.shape, jnp.float32),
        grid=(n_tok // POOL_TM,),
        in_specs=[
            pl.BlockSpec((POOL_TM, D_MODEL), lambda i: (i, 0)),
            pl.BlockSpec((POOL_HALO, D_MODEL), lambda i: (jnp.maximum(i * halo_blocks_per_tile - 1, 0), 0)),
            pl.BlockSpec((1, 1, 3 * D_MODEL), lambda i: (mod_idx + i // tiles_per_seq, 0, 0)),
            pl.BlockSpec((1, 1, D_MODEL), lambda i: (norm_idx, 0, 0)),
            pl.BlockSpec((None, n_grp, POOL_GROUP, POOL_GROUP), lambda i: (pool_idx, 0, 0, 0)),
            pl.BlockSpec((1, 1, D_MODEL), lambda i: (pool_idx, 0, 0)),
        ],
        out_specs=pl.BlockSpec((POOL_TM, D_MODEL), lambda i: (i, 0)),
        scratch_shapes=[pltpu.VMEM((n_grp, POOL_GROUP, POOL_GROUP), jnp.bfloat16)],
        compiler_params=_params("arbitrary"),
        name="pool_mixer",
    )(x2d, x2d, mod, norm_g, pool_w, pool_scale)


def _tile_copies(x_hbm, x_ref, sems, tile):
    slots, tm, _ = x_ref.shape
    slot = tile % slots
    first_row = pl.multiple_of(tile * tm, tm)
    return [
        pltpu.make_async_copy(x_hbm.at[pl.ds(first_row + k * ROW_CHUNK, ROW_CHUNK), :],
                              x_ref.at[slot, pl.ds(k * ROW_CHUNK, ROW_CHUNK), :], sems.at[slot, k])
        for k in range(tm // ROW_CHUNK)
    ]


def _request_tile(x_hbm, x_ref, sems, tile, when):
    @pl.when(when)
    def _():
        for cp in _tile_copies(x_hbm, x_ref, sems, tile):
            cp.start()


def _await_tile(x_hbm, x_ref, sems, tile):
    for cp in _tile_copies(x_hbm, x_ref, sems, tile):
        cp.wait()
    return x_ref.at[tile % x_ref.shape[0]]


def _normalise_into_first_product(x_tile, g, mod_row, h_ref, weight_rows):
    x = x_tile[...]
    inv = lax.rsqrt(jnp.mean(x * x, axis=-1, keepdims=True) + EPS)
    col_gain = g * (1.0 + mod_row[:, D_MODEL:2 * D_MODEL])
    acc = None
    for lo in range(0, D_MODEL, NORM_COLS):
        cols = slice(lo, lo + NORM_COLS)
        hk = (x_tile[:, cols] * inv * col_gain[:, cols] + mod_row[:, cols]).astype(jnp.bfloat16)
        h_ref[:, cols] = hk
        part = jnp.dot(hk, weight_rows(cols), preferred_element_type=jnp.float32)
        acc = part if acc is None else acc + part
    return acc


def _mlp_kernel(last, x_hbm, mod_ref, g_ref, w1_ref, w2_ref, *refs):
    if last:
        fg_ref, o_ref, h_ref, x_ref, sems = refs
    else:
        wa_ref, ba_ref, cond_ref, o_ref, next_mod_ref, h_ref, x_ref, sems = refs
    i = pl.program_id(0)
    j = pl.program_id(1)

    def finish(pre, base_ref):
        a = jnp.maximum(pre, 0.0)
        a = (a * a).astype(jnp.bfloat16)
        w2 = w2_ref[...].astype(jnp.bfloat16)
        for n in range(0, D_MODEL, MLP_TN):
            cols = slice(n, n + MLP_TN)
            gate = mod_ref[0, :, 2 * D_MODEL + n:2 * D_MODEL + n + MLP_TN]
            o_ref[:, cols] = base_ref[:, cols] + gate * jnp.dot(a, w2[:, cols], preferred_element_type=jnp.float32)
        if not last:
            next_mod_ref[...] = _ada_columns(wa_ref[...], cond_ref, ba_ref[...])

    @pl.when(j == 0)
    def _():
        _request_tile(x_hbm, x_ref, sems, i, i == 0)
        _request_tile(x_hbm, x_ref, sems, i + 1, i + 1 < pl.num_programs(0))
        x_tile = _await_tile(x_hbm, x_ref, sems, i)
        pre = _normalise_into_first_product(x_tile, g_ref[0], mod_ref[0], h_ref,
                                            lambda rows: w1_ref[rows, :].astype(jnp.bfloat16))
        finish(pre, x_tile)

    @pl.when(j > 0)
    def _():
        finish(jnp.dot(h_ref[...], w1_ref[...].astype(jnp.bfloat16), preferred_element_type=jnp.float32), o_ref)

    if last:
        @pl.when(j == pl.num_programs(1) - 1)
        def _():
            y = o_ref[...]
            inv = lax.rsqrt(jnp.mean(y * y, axis=-1, keepdims=True) + EPS)
            o_ref[...] = y * inv * fg_ref[...]


def _mlp_layer(x2d, mod, mod_idx, seq, norm_g, norm_idx, w1, w2, layer, final_g=None, ada=None):
    n_tok = x2d.shape[0]
    tiles_per_seq = seq // MLP_TM
    steps = D_FF // MLP_TF
    last = ada is None
    in_specs = [
        pl.BlockSpec(memory_space=pl.ANY),
        pl.BlockSpec((1, 1, 3 * D_MODEL), lambda i, j: (mod_idx + i // tiles_per_seq, 0, 0)),
        pl.BlockSpec((1, 1, D_MODEL), lambda i, j: (norm_idx, 0, 0)),
        pl.BlockSpec((None, D_MODEL, MLP_TF), lambda i, j: (layer, 0, j)),
        pl.BlockSpec((None, MLP_TF, D_MODEL), lambda i, j: (layer, j, 0)),
    ]
    x_spec = pl.BlockSpec((MLP_TM, D_MODEL), lambda i, j: (i, 0))
    x_shape = jax.ShapeDtypeStruct(x2d.shape, jnp.float32)
    if last:
        operands = (x2d, mod, norm_g, w1, w2, final_g)
        in_specs.append(pl.BlockSpec((1, D_MODEL), lambda i, j: (0, 0)))
        out_specs, out_shape = x_spec, x_shape
    else:
        ada_w, ada_b, cond = operands_ada = ada
        batch = cond.shape[0]
        blocks_per_row = 3 * D_MODEL // ADA_FUSED_TK
        n_blocks = 2 * blocks_per_row
        assert n_blocks <= (n_tok // MLP_TM) * steps

        def ada_block(i, j):
            blk = jnp.minimum(i * steps + j, n_blocks - 1)
            return 2 * (layer + 1) + blk // blocks_per_row, 0, blk % blocks_per_row

        def next_mod_block(i, j):
            row, _, col = ada_block(i, j)
            return row - 2 * (layer + 1), 0, col

        operands = (x2d, mod, norm_g, w1, w2) + operands_ada
        in_specs += [
            pl.BlockSpec((None, D_MODEL, ADA_FUSED_TK), ada_block),
            pl.BlockSpec((None, 1, ADA_FUSED_TK), ada_block),
            pl.BlockSpec(cond.shape, lambda i, j: (0, 0, 0)),
        ]
        out_specs = (x_spec, pl.BlockSpec((None, batch, ADA_FUSED_TK), next_mod_block))
        out_shape = (x_shape, jax.ShapeDtypeStruct((2, batch, 3 * D_MODEL), jnp.float32))
    return pl.pallas_call(
        functools.partial(_mlp_kernel, last),
        out_shape=out_shape,
        grid=(n_tok // MLP_TM, steps),
        in_specs=in_specs,
        out_specs=out_specs,
        scratch_shapes=[pltpu.VMEM((MLP_TM, D_MODEL), jnp.bfloat16),
                        pltpu.VMEM((2, MLP_TM, D_MODEL), jnp.float32),
                        pltpu.SemaphoreType.DMA((2, MLP_TM // ROW_CHUNK))],
        compiler_params=_params("arbitrary", "arbitrary"),
        name="mlp_final" if last else "mlp",
    )(*operands)


def _proj_kernel(x_hbm, mod_ref, g_ref, wqk_ref, wvg_ref, cos_ref, sin_ref, oqk_ref, ovg_ref, h_ref, x_ref, sems):
    i = pl.program_id(0)
    j = pl.program_id(1)
    n_q = RET_QK_DIM // PROJ_QK_TN

    def weights(rows=slice(None)):
        return jnp.concatenate([wqk_ref[rows, :].astype(jnp.bfloat16), wvg_ref[rows, :].astype(jnp.bfloat16)], axis=1)

    def finish(r):
        k_scale = jnp.where(j >= n_q, RET_HEAD_QK ** -0.5, 1.0)
        cos = cos_ref[...] * k_scale
        sin = sin_ref[...] * k_scale
        swap = lax.broadcasted_iota(jnp.int32, (PROJ_TM, LANES), 1) ^ 1
        for lo in range(0, PROJ_QK_TN, LANES):
            t = lo % RET_HEAD_QK
            xs = r[:, lo:lo + LANES]
            partner = jnp.take_along_axis(xs, swap, axis=1)
            oqk_ref[lo // PROJ_GROUP, :, t:t + LANES] = (
                xs * cos[:, t:t + LANES] + partner * sin[:, t:t + LANES]).astype(oqk_ref.dtype)
        for grp in range(PROJ_VG_TN // PROJ_GROUP):
            lo = PROJ_QK_TN + grp * PROJ_GROUP
            ovg_ref[grp] = r[:, lo:lo + PROJ_GROUP].astype(ovg_ref.dtype)

    @pl.when(j == 0)
    def _():
        _request_tile(x_hbm, x_ref, sems, i, i == 0)
        x_tile = _await_tile(x_hbm, x_ref, sems, i)
        finish(_normalise_into_first_product(x_tile, g_ref[0], mod_ref[0], h_ref, weights))

    _request_tile(x_hbm, x_ref, sems, i + 1, (j == 1) & (i + 1 < pl.num_programs(0)))

    @pl.when(j > 0)
    def _():
        finish(jnp.dot(h_ref[...], weights(), preferred_element_type=jnp.float32))


def _ret_projection(x2d, mod, mod_idx, seq, norm_g, norm_idx, w_in, layer, cos, sin):
    n_tok = x2d.shape[0]
    tiles_per_seq = seq // PROJ_TM
    steps = 2 * RET_QK_DIM // PROJ_QK_TN
    assert steps == 2 * RET_V_DIM // PROJ_VG_TN
    vg_off = 2 * RET_QK_DIM // PROJ_VG_TN
    return pl.pallas_call(
        _proj_kernel,
        out_shape=(jax.ShapeDtypeStruct((2 * RET_QK_DIM // PROJ_GROUP, n_tok, PROJ_GROUP), jnp.bfloat16),
                   jax.ShapeDtypeStruct((2 * RET_V_DIM // PROJ_GROUP, n_tok, PROJ_GROUP), jnp.bfloat16)),
        grid=(n_tok // PROJ_TM, steps),
        in_specs=[
            pl.BlockSpec(memory_space=pl.ANY),
            pl.BlockSpec((1, 1, 3 * D_MODEL), lambda i, j: (mod_idx + i // tiles_per_seq, 0, 0)),
            pl.BlockSpec((1, 1, D_MODEL), lambda i, j: (norm_idx, 0, 0)),
            pl.BlockSpec((None, D_MODEL, PROJ_QK_TN), lambda i, j: (layer, 0, j)),
            pl.BlockSpec((None, D_MODEL, PROJ_VG_TN), lambda i, j: (layer, 0, vg_off + j)),
            pl.BlockSpec((PROJ_TM, RET_HEAD_QK), lambda i, j: (i % tiles_per_seq, 0)),
            pl.BlockSpec((PROJ_TM, RET_HEAD_QK), lambda i, j: (i % tiles_per_seq, 0)),
        ],
        out_specs=(pl.BlockSpec((PROJ_QK_TN // PROJ_GROUP, PROJ_TM, PROJ_GROUP), lambda i, j: (j, i, 0)),
                   pl.BlockSpec((PROJ_VG_TN // PROJ_GROUP, PROJ_TM, PROJ_GROUP), lambda i, j: (j, i, 0))),
        scratch_shapes=[pltpu.VMEM((PROJ_TM, D_MODEL), jnp.bfloat16),
                        pltpu.VMEM((1, PROJ_TM, D_MODEL), jnp.float32),
                        pltpu.SemaphoreType.DMA((1, PROJ_TM // ROW_CHUNK))],
        compiler_params=_params("arbitrary", "arbitrary"),
        name="ret_projection",
    )(x2d, mod, norm_g, w_in, w_in, cos, sin)


def _scan_kernel(q_ref, k_ref, v_ref, g_ref, mask_ref, eps_ref, kdec_ref, cdec_ref, o_ref, state_ref):
    @pl.when(pl.program_id(2) == 0)
    def _():
        state_ref[...] = jnp.zeros_like(state_ref)

    mask = mask_ref[...]
    row_eps = eps_ref[...]
    k_dec = kdec_ref[...]
    chunk_dec = cdec_ref[...]
    state = state_ref[...]
    for c in range(q_ref.shape[0] // RET_CHUNK):
        rows = slice(c * RET_CHUNK, (c + 1) * RET_CHUNK)
        qc = q_ref[rows, :]
        kc = k_ref[rows, :]
        vc = jnp.concatenate([v_ref[grp, rows, :] for grp in range(v_ref.shape[0])], axis=1)
        gc = jnp.concatenate([g_ref[grp, rows, :] for grp in range(g_ref.shape[0])], axis=1)
        scores = lax.dot_general(qc, kc, (((1,), (1,)), ((), ())), preferred_element_type=jnp.float32)
        scores = (scores * mask).astype(jnp.bfloat16)
        y = (jnp.dot(scores, vc, preferred_element_type=jnp.float32)
             + jnp.dot(qc, state.astype(jnp.bfloat16), preferred_element_type=jnp.float32))
        k_decayed = (kc.astype(jnp.float32) * k_dec).astype(jnp.bfloat16)
        state = chunk_dec * state + lax.dot_general(
            k_decayed, vc, (((0,), (0,)), ((), ())), preferred_element_type=jnp.float32)
        mu = jnp.mean(y, axis=-1, keepdims=True)
        yc = y - mu
        var = jnp.mean(yc * yc, axis=-1, keepdims=True)
        yn = yc * lax.rsqrt(var + row_eps)
        z = (jax.nn.silu(gc.astype(jnp.float32)) * yn).astype(o_ref.dtype)
        for grp in range(o_ref.shape[0]):
            o_ref[grp, rows, :] = z[:, grp * PROJ_GROUP:(grp + 1) * PROJ_GROUP]
    state_ref[...] = state


def _ret_scan(proj_qk, proj_vg, batch, seq, mask, row_eps, k_dec, chunk_dec):
    assert RET_HEAD_QK == PROJ_GROUP
    n_tok = proj_qk.shape[1]
    steps = seq // RET_TS
    v_groups = RET_HEAD_V // PROJ_GROUP
    k_off = RET_HEADS
    g_off = RET_HEADS
    return pl.pallas_call(
        _scan_kernel,
        out_shape=jax.ShapeDtypeStruct((RET_V_DIM // PROJ_GROUP, n_tok, PROJ_GROUP), jnp.bfloat16),
        grid=(batch, RET_HEADS, steps),
        in_specs=[
            pl.BlockSpec((None, RET_TS, PROJ_GROUP), lambda b, h, t: (h, b * steps + t, 0)),
            pl.BlockSpec((None, RET_TS, PROJ_GROUP), lambda b, h, t: (k_off + h, b * steps + t, 0)),
            pl.BlockSpec((v_groups, RET_TS, PROJ_GROUP), lambda b, h, t: (h, b * steps + t, 0)),
            pl.BlockSpec((v_groups, RET_TS, PROJ_GROUP), lambda b, h, t: (g_off + h, b * steps + t, 0)),
            pl.BlockSpec((None, RET_CHUNK, RET_CHUNK), lambda b, h, t: (h, 0, 0)),
            pl.BlockSpec((None, RET_CHUNK, 1), lambda b, h, t: (h, 0, 0)),
            pl.BlockSpec((None, RET_CHUNK, 1), lambda b, h, t: (h, 0, 0)),
            pl.BlockSpec((None, 1, 1), lambda b, h, t: (h, 0, 0)),
        ],
        out_specs=pl.BlockSpec((v_groups, RET_TS, PROJ_GROUP), lambda b, h, t: (h, b * steps + t, 0)),
        scratch_shapes=[pltpu.VMEM((RET_HEAD_QK, RET_HEAD_V), jnp.float32)],
        compiler_params=_params("parallel", "parallel", "arbitrary"),
        name="ret_scan",
    )(proj_qk, proj_qk, proj_vg, proj_vg, mask, row_eps, k_dec, chunk_dec)


def _out_kernel(z_ref, w_ref, x_ref, gate_ref, o_ref, wb_ref):
    j = pl.program_id(1)

    @pl.when(pl.program_id(0) == 0)
    def _():
        wb_ref[j] = w_ref[...].astype(jnp.bfloat16)

    y = None
    for grp in range(z_ref.shape[0]):
        part = jnp.dot(z_ref[grp], wb_ref[j, grp * PROJ_GROUP:(grp + 1) * PROJ_GROUP, :],
                       preferred_element_type=jnp.float32)
        y = part if y is None else y + part
    o_ref[...] = x_ref[...] + gate_ref[0] * y


def _ret_output(z, x2d, mod, mod_idx, seq, w_out, layer):
    n_tok = x2d.shape[0]
    tiles_per_seq = seq // OUT_TM
    n_col = D_MODEL // OUT_TN
    gate_off = 2 * D_MODEL // OUT_TN
    return pl.pallas_call(
        _out_kernel,
        out_shape=jax.ShapeDtypeStruct(x2d.shape, jnp.float32),
        grid=(n_tok // OUT_TM, n_col),
        in_specs=[
            pl.BlockSpec((RET_V_DIM // PROJ_GROUP, OUT_TM, PROJ_GROUP), lambda i, j: (0, i, 0)),
            pl.BlockSpec((None, RET_V_DIM, OUT_TN), lambda i, j: (layer, 0, jnp.where(i == 0, j, n_col - 1))),
            pl.BlockSpec((OUT_TM, OUT_TN), lambda i, j: (i, j)),
            pl.BlockSpec((1, 1, OUT_TN), lambda i, j: (mod_idx + i // tiles_per_seq, 0, gate_off + j)),
        ],
        out_specs=pl.BlockSpec((OUT_TM, OUT_TN), lambda i, j: (i, j)),
        scratch_shapes=[pltpu.VMEM((n_col, RET_V_DIM, OUT_TN), jnp.bfloat16)],
        compiler_params=_params("arbitrary", "arbitrary"),
        name="ret_output",
    )(z, w_out, x2d, mod)


def _rotary_tables(seq):
    inv = ROPE_BASE ** (-np.arange(0, RET_HEAD_QK, 2, dtype=np.float64) / RET_HEAD_QK)
    ang = np.arange(seq, dtype=np.float64)[:, None] * inv[None, :]
    cos = np.repeat(np.cos(ang), 2, axis=1)
    sin = np.stack([-np.sin(ang), np.sin(ang)], axis=-1).reshape(seq, RET_HEAD_QK)
    return cos.astype(np.float32), sin.astype(np.float32)


def _decay_tables():
    gamma = 1.0 - 2.0 ** (-5.0 - np.arange(RET_HEADS, dtype=np.float64))
    log_g = np.log(gamma)
    j = np.arange(RET_CHUNK, dtype=np.float64)
    causal = j[:, None] >= j[None, :]
    mask = np.where(causal[None], np.exp(-(j[None, None, :] + 1.0) * log_g[:, None, None]), 0.0)
    row_eps = (EPS * np.exp(-2.0 * (j[None, :] + 1.0) * log_g[:, None]))[:, :, None]
    k_dec = np.exp((RET_CHUNK - 1.0 - j[None, :]) * log_g[:, None])[:, :, None]
    chunk_dec = np.exp(RET_CHUNK * log_g)[:, None, None]
    return tuple(t.astype(np.float32) for t in (mask, row_eps, k_dec, chunk_dec))


@jax.jit
def kernel(x, c, ada_w, ada_b, norm_g, pool_w, pool_scale, ret_w_in, ret_w_out, mlp_w1, mlp_w2, final_g):
    batch, seq, d = x.shape
    assert d == D_MODEL
    assert seq % max(POOL_TM, MLP_TM, PROJ_TM, OUT_TM, RET_TS) == 0

    ada_w = ada_w.reshape(DEPTH * 2, d, 3 * d)
    ada_b = ada_b.reshape(DEPTH * 2, 1, 3 * d)
    c_lanes = jnp.broadcast_to(c[:, :, None], (batch, d, LANES))
    mod, cond = _ada_first_layer(c_lanes, ada_w, ada_b)

    cos, sin = _rotary_tables(seq)
    mask, row_eps, k_dec, chunk_dec = _decay_tables()
    norm_rows = norm_g.reshape(DEPTH * 2, 1, d)
    scale_rows = pool_scale.reshape(-1, 1, d)

    x2d = x.reshape(batch * seq, d)
    for i in range(DEPTH):
        mod = mod.reshape(2 * batch, 1, 3 * d)
        mix_idx, mlp_idx = 0, batch
        if i % 2 == 0:
            x2d = _pool_layer(x2d, mod, mix_idx, seq, norm_rows, 2 * i, pool_w, scale_rows, i // 2)
        else:
            proj_qk, proj_vg = _ret_projection(x2d, mod, mix_idx, seq, norm_rows, 2 * i, ret_w_in, i // 2, cos, sin)
            z = _ret_scan(proj_qk, proj_vg, batch, seq, mask, row_eps, k_dec, chunk_dec)
            x2d = _ret_output(z, x2d, mod, mix_idx, seq, ret_w_out, i // 2)
        if i < DEPTH - 1:
            x2d, mod = _mlp_layer(x2d, mod, mlp_idx, seq, norm_rows, 2 * i + 1, mlp_w1, mlp_w2, i,
                                  ada=(ada_w, ada_b, cond))
        else:
            x2d = _mlp_layer(x2d, mod, mlp_idx, seq, norm_rows, 2 * i + 1, mlp_w1, mlp_w2, i,
                             final_g=final_g.reshape(1, d))
    return x2d.reshape(batch, seq, d)
```

```python
import functools

import jax
import jax.numpy as jnp
import numpy as np
from jax import lax
from jax.experimental import pallas as pl
from jax.experimental.pallas import tpu as pltpu

D_MODEL = 2048
DEPTH = 4
POOL_WINDOWS = (2, 4, 8, 16)
POOL_GROUP = D_MODEL // len(POOL_WINDOWS)
POOL_HALO = 16
RET_HEADS = 8
RET_QK_DIM = D_MODEL
RET_V_DIM = 2 * D_MODEL
RET_HEAD_QK = RET_QK_DIM // RET_HEADS
RET_HEAD_V = RET_V_DIM // RET_HEADS
RET_PROJ = 2 * RET_QK_DIM + 2 * RET_V_DIM
ROPE_BASE = 10000.0
D_FF = 4 * D_MODEL
EPS = 1e-6
LANES = 128

V7X_VMEM_BYTES = 64 * 1024 * 1024
VMEM_LIMIT_BYTES = V7X_VMEM_BYTES - 4 * 1024 * 1024

ROW_CHUNK = 128
NORM_COLS = 256
ADA_TK = 1024
ADA_FUSED_TK = 128
POOL_TM = 1024
MLP_TM = 1024
MLP_TF = 512
MLP_TN = 512
PROJ_TM = 1024
PROJ_QK_TN = 512
PROJ_VG_TN = 1024
PROJ_GROUP = 256
RET_TS = 4096
RET_CHUNK = 256
OUT_TM = 1024
OUT_TN = 512


def _params(*semantics):
    return pltpu.CompilerParams(dimension_semantics=semantics, vmem_limit_bytes=VMEM_LIMIT_BYTES)


def _norm_modulate(x, g, mod_row):
    shift = mod_row[:, 0:D_MODEL]
    scale = mod_row[:, D_MODEL:2 * D_MODEL]
    inv = lax.rsqrt(jnp.mean(x * x, axis=-1, keepdims=True) + EPS)
    return x * inv * (g * (1.0 + scale)) + shift


def _ada_columns(w, cond_ref, bias):
    rows = []
    for b in range(cond_ref.shape[0]):
        cond = cond_ref[b]
        rows.append(jnp.concatenate(
            [jnp.sum(w[:, lo:lo + LANES] * cond, axis=0, keepdims=True) for lo in range(0, w.shape[1], LANES)],
            axis=1))
    return jnp.concatenate(rows, axis=0) + bias


def _ada_kernel(c_ref, w_ref, b_ref, o_ref, cond_ref):
    cond_ref[...] = jax.nn.silu(c_ref[...])
    o_ref[...] = _ada_columns(w_ref[...], cond_ref, b_ref[...])


def _ada_first_layer(c_lanes, ada_w, ada_b):
    batch = c_lanes.shape[0]
    n_out = ada_w.shape[2]
    return pl.pallas_call(
        _ada_kernel,
        out_shape=(jax.ShapeDtypeStruct((2, batch, n_out), jnp.float32),
                   jax.ShapeDtypeStruct(c_lanes.shape, jnp.float32)),
        grid=(2, n_out // ADA_TK),
        in_specs=[
            pl.BlockSpec(c_lanes.shape, lambda l, j: (0, 0, 0)),
            pl.BlockSpec((None, D_MODEL, ADA_TK), lambda l, j: (l, 0, j)),
            pl.BlockSpec((None, 1, ADA_TK), lambda l, j: (l, 0, j)),
        ],
        out_specs=(pl.BlockSpec((None, batch, ADA_TK), lambda l, j: (l, 0, j)),
                   pl.BlockSpec(c_lanes.shape, lambda l, j: (0, 0, 0))),
        compiler_params=_params("arbitrary", "arbitrary"),
        name="ada_modulation",
    )(c_lanes, ada_w, ada_b)


def _pool_kernel(tiles_per_seq, x_ref, halo_ref, mod_ref, g_ref, w_ref, ls_ref, o_ref, wb_ref):
    i = pl.program_id(0)

    @pl.when(i == 0)
    def _():
        wb_ref[...] = w_ref[...].astype(jnp.bfloat16)

    tile_in_seq = i % tiles_per_seq
    tm = x_ref.shape[0]
    mod_row = mod_ref[0]
    g = g_ref[0]
    x = x_ref[...]
    h = _norm_modulate(x, g, mod_row)
    h_halo = _norm_modulate(halo_ref[...], g, mod_row)
    h_halo = jnp.where(tile_in_seq == 0, 0.0, h_halo)
    out_scale = mod_row[:, 2 * D_MODEL:3 * D_MODEL] * ls_ref[0]
    pos = tile_in_seq * tm + lax.broadcasted_iota(jnp.int32, (tm, 1), 0)
    for gi, window in enumerate(POOL_WINDOWS):
        cols = slice(gi * POOL_GROUP, (gi + 1) * POOL_GROUP)
        hg = h[:, cols]
        acc = jnp.concatenate([h_halo[:, cols], hg], axis=0)
        shift = 1
        while shift < window:
            acc = acc + pltpu.roll(acc, shift, axis=0)
            shift *= 2
        cnt = jnp.minimum(pos + 1, window).astype(jnp.float32)
        p = acc[POOL_HALO:, :] / cnt - hg
        y = jnp.dot(p.astype(jnp.bfloat16), wb_ref[gi], preferred_element_type=jnp.float32)
        o_ref[:, cols] = x[:, cols] + out_scale[:, cols] * y


def _pool_layer(x2d, mod, mod_idx, seq, norm_g, norm_idx, pool_w, pool_scale, pool_idx):
    n_tok = x2d.shape[0]
    tiles_per_seq = seq // POOL_TM
    halo_blocks_per_tile = POOL_TM // POOL_HALO
    n_grp = len(POOL_WINDOWS)
    return pl.pallas_call(
        functools.partial(_pool_kernel, tiles_per_seq),
        out_shape=jax.ShapeDtypeStruct(x2d.shape, jnp.float32),
        grid=(n_tok // POOL_TM,),
        in_specs=[
            pl.BlockSpec((POOL_TM, D_MODEL), lambda i: (i, 0)),
            pl.BlockSpec((POOL_HALO, D_MODEL), lambda i: (jnp.maximum(i * halo_blocks_per_tile - 1, 0), 0)),
            pl.BlockSpec((1, 1, 3 * D_MODEL), lambda i: (mod_idx + i // tiles_per_seq, 0, 0)),
            pl.BlockSpec((1, 1, D_MODEL), lambda i: (norm_idx, 0, 0)),
            pl.BlockSpec((None, n_grp, POOL_GROUP, POOL_GROUP), lambda i: (pool_idx, 0, 0, 0)),
            pl.BlockSpec((1, 1, D_MODEL), lambda i: (pool_idx, 0, 0)),
        ],
        out_specs=pl.BlockSpec((POOL_TM, D_MODEL), lambda i: (i, 0)),
        scratch_shapes=[pltpu.VMEM((n_grp, POOL_GROUP, POOL_GROUP), jnp.bfloat16)],
        compiler_params=_params("arbitrary"),
        name="pool_mixer",
    )(x2d, x2d, mod, norm_g, pool_w, pool_scale)


def _tile_copies(x_hbm, x_ref, sems, tile):
    slots, tm, _ = x_ref.shape
    slot = tile % slots
    first_row = pl.multiple_of(tile * tm, tm)
    return [
        pltpu.make_async_copy(x_hbm.at[pl.ds(first_row + k * ROW_CHUNK, ROW_CHUNK), :],
                              x_ref.at[slot, pl.ds(k * ROW_CHUNK, ROW_CHUNK), :], sems.at[slot, k])
        for k in range(tm // ROW_CHUNK)
    ]


def _request_tile(x_hbm, x_ref, sems, tile, when):
    @pl.when(when)
    def _():
        for cp in _tile_copies(x_hbm, x_ref, sems, tile):
            cp.start()


def _await_tile(x_hbm, x_ref, sems, tile):
    for cp in _tile_copies(x_hbm, x_ref, sems, tile):
        cp.wait()
    return x_ref.at[tile % x_ref.shape[0]]


def _normalise_into_first_product(x_tile, g, mod_row, h_ref, weight_rows):
    x = x_tile[...]
    inv = lax.rsqrt(jnp.mean(x * x, axis=-1, keepdims=True) + EPS)
    col_gain = g * (1.0 + mod_row[:, D_MODEL:2 * D_MODEL])
    acc = None
    for lo in range(0, D_MODEL, NORM_COLS):
        cols = slice(lo, lo + NORM_COLS)
        hk = (x_tile[:, cols] * inv * col_gain[:, cols] + mod_row[:, cols]).astype(jnp.bfloat16)
        h_ref[:, cols] = hk
        part = jnp.dot(hk, weight_rows(cols), preferred_element_type=jnp.float32)
        acc = part if acc is None else acc + part
    return acc


def _mlp_kernel(last, x_hbm, mod_ref, g_ref, w1_ref, w2_ref, *refs):
    if last:
        fg_ref, o_ref, h_ref, x_ref, sems = refs
    else:
        wa_ref, ba_ref, cond_ref, o_ref, next_mod_ref, h_ref, x_ref, sems = refs
    i = pl.program_id(0)
    j = pl.program_id(1)

    def finish(pre, base_ref):
        a = jnp.maximum(pre, 0.0)
        a = (a * a).astype(jnp.bfloat16)
        w2 = w2_ref[...].astype(jnp.bfloat16)
        for n in range(0, D_MODEL, MLP_TN):
            cols = slice(n, n + MLP_TN)
            gate = mod_ref[0, :, 2 * D_MODEL + n:2 * D_MODEL + n + MLP_TN]
            o_ref[:, cols] = base_ref[:, cols] + gate * jnp.dot(a, w2[:, cols], preferred_element_type=jnp.float32)
        if not last:
            next_mod_ref[...] = _ada_columns(wa_ref[...], cond_ref, ba_ref[...])

    @pl.when(j == 0)
    def _():
        _request_tile(x_hbm, x_ref, sems, i, i == 0)
        x_tile = _await_tile(x_hbm, x_ref, sems, i)
        pre = _normalise_into_first_product(x_tile, g_ref[0], mod_ref[0], h_ref,
                                            lambda rows: w1_ref[rows, :].astype(jnp.bfloat16))
        finish(pre, x_tile)
        _request_tile(x_hbm, x_ref, sems, i + 1, i + 1 < pl.num_programs(0))

    @pl.when(j > 0)
    def _():
        finish(jnp.dot(h_ref[...], w1_ref[...].astype(jnp.bfloat16), preferred_element_type=jnp.float32), o_ref)

    if last:
        @pl.when(j == pl.num_programs(1) - 1)
        def _():
            y = o_ref[...]
            inv = lax.rsqrt(jnp.mean(y * y, axis=-1, keepdims=True) + EPS)
            o_ref[...] = y * inv * fg_ref[...]


def _mlp_layer(x2d, mod, mod_idx, seq, norm_g, norm_idx, w1, w2, layer, final_g=None, ada=None):
    n_tok = x2d.shape[0]
    tiles_per_seq = seq // MLP_TM
    steps = D_FF // MLP_TF
    last = ada is None
    in_specs = [
        pl.BlockSpec(memory_space=pl.ANY),
        pl.BlockSpec((1, 1, 3 * D_MODEL), lambda i, j: (mod_idx + i // tiles_per_seq, 0, 0)),
        pl.BlockSpec((1, 1, D_MODEL), lambda i, j: (norm_idx, 0, 0)),
        pl.BlockSpec((None, D_MODEL, MLP_TF), lambda i, j: (layer, 0, j)),
        pl.BlockSpec((None, MLP_TF, D_MODEL), lambda i, j: (layer, j, 0)),
    ]
    x_spec = pl.BlockSpec((MLP_TM, D_MODEL), lambda i, j: (i, 0))
    x_shape = jax.ShapeDtypeStruct(x2d.shape, jnp.float32)
    if last:
        operands = (x2d, mod, norm_g, w1, w2, final_g)
        in_specs.append(pl.BlockSpec((1, D_MODEL), lambda i, j: (0, 0)))
        out_specs, out_shape = x_spec, x_shape
    else:
        ada_w, ada_b, cond = operands_ada = ada
        batch = cond.shape[0]
        blocks_per_row = 3 * D_MODEL // ADA_FUSED_TK
        n_blocks = 2 * blocks_per_row
        assert n_blocks <= (n_tok // MLP_TM) * steps

        def ada_block(i, j):
            blk = jnp.minimum(i * steps + j, n_blocks - 1)
            return 2 * (layer + 1) + blk // blocks_per_row, 0, blk % blocks_per_row

        def next_mod_block(i, j):
            row, _, col = ada_block(i, j)
            return row - 2 * (layer + 1), 0, col

        operands = (x2d, mod, norm_g, w1, w2) + operands_ada
        in_specs += [
            pl.BlockSpec((None, D_MODEL, ADA_FUSED_TK), ada_block),
            pl.BlockSpec((None, 1, ADA_FUSED_TK), ada_block),
            pl.BlockSpec(cond.shape, lambda i, j: (0, 0, 0)),
        ]
        out_specs = (x_spec, pl.BlockSpec((None, batch, ADA_FUSED_TK), next_mod_block))
        out_shape = (x_shape, jax.ShapeDtypeStruct((2, batch, 3 * D_MODEL), jnp.float32))
    return pl.pallas_call(
        functools.partial(_mlp_kernel, last),
        out_shape=out_shape,
        grid=(n_tok // MLP_TM, steps),
        in_specs=in_specs,
        out_specs=out_specs,
        scratch_shapes=[pltpu.VMEM((MLP_TM, D_MODEL), jnp.bfloat16),
                        pltpu.VMEM((2, MLP_TM, D_MODEL), jnp.float32),
                        pltpu.SemaphoreType.DMA((2, MLP_TM // ROW_CHUNK))],
        compiler_params=_params("arbitrary", "arbitrary"),
        name="mlp_final" if last else "mlp",
    )(*operands)


def _proj_kernel(x_hbm, mod_ref, g_ref, wqk_ref, wvg_ref, cos_ref, sin_ref, oqk_ref, ovg_ref, h_ref, x_ref, sems):
    i = pl.program_id(0)
    j = pl.program_id(1)
    n_q = RET_QK_DIM // PROJ_QK_TN

    def weights(rows=slice(None)):
        return jnp.concatenate([wqk_ref[rows, :].astype(jnp.bfloat16), wvg_ref[rows, :].astype(jnp.bfloat16)], axis=1)

    def finish(r):
        k_scale = jnp.where(j >= n_q, RET_HEAD_QK ** -0.5, 1.0)
        cos = cos_ref[...] * k_scale
        sin = sin_ref[...] * k_scale
        swap = lax.broadcasted_iota(jnp.int32, (PROJ_TM, LANES), 1) ^ 1
        for lo in range(0, PROJ_QK_TN, LANES):
            t = lo % RET_HEAD_QK
            xs = r[:, lo:lo + LANES]
            partner = jnp.take_along_axis(xs, swap, axis=1)
            oqk_ref[lo // PROJ_GROUP, :, t:t + LANES] = (
                xs * cos[:, t:t + LANES] + partner * sin[:, t:t + LANES]).astype(oqk_ref.dtype)
        for grp in range(PROJ_VG_TN // PROJ_GROUP):
            lo = PROJ_QK_TN + grp * PROJ_GROUP
            ovg_ref[grp] = r[:, lo:lo + PROJ_GROUP].astype(ovg_ref.dtype)

    @pl.when(j == 0)
    def _():
        _request_tile(x_hbm, x_ref, sems, i, i == 0)
        x_tile = _await_tile(x_hbm, x_ref, sems, i)
        finish(_normalise_into_first_product(x_tile, g_ref[0], mod_ref[0], h_ref, weights))
        _request_tile(x_hbm, x_ref, sems, i + 1, i + 1 < pl.num_programs(0))

    @pl.when(j > 0)
    def _():
        finish(jnp.dot(h_ref[...], weights(), preferred_element_type=jnp.float32))


def _ret_projection(x2d, mod, mod_idx, seq, norm_g, norm_idx, w_in, layer, cos, sin):
    n_tok = x2d.shape[0]
    tiles_per_seq = seq // PROJ_TM
    steps = 2 * RET_QK_DIM // PROJ_QK_TN
    assert steps == 2 * RET_V_DIM // PROJ_VG_TN
    vg_off = 2 * RET_QK_DIM // PROJ_VG_TN
    return pl.pallas_call(
        _proj_kernel,
        out_shape=(jax.ShapeDtypeStruct((2 * RET_QK_DIM // PROJ_GROUP, n_tok, PROJ_GROUP), jnp.bfloat16),
                   jax.ShapeDtypeStruct((2 * RET_V_DIM // PROJ_GROUP, n_tok, PROJ_GROUP), jnp.bfloat16)),
        grid=(n_tok // PROJ_TM, steps),
        in_specs=[
            pl.BlockSpec(memory_space=pl.ANY),
            pl.BlockSpec((1, 1, 3 * D_MODEL), lambda i, j: (mod_idx + i // tiles_per_seq, 0, 0)),
            pl.BlockSpec((1, 1, D_MODEL), lambda i, j: (norm_idx, 0, 0)),
            pl.BlockSpec((None, D_MODEL, PROJ_QK_TN), lambda i, j: (layer, 0, j)),
            pl.BlockSpec((None, D_MODEL, PROJ_VG_TN), lambda i, j: (layer, 0, vg_off + j)),
            pl.BlockSpec((PROJ_TM, RET_HEAD_QK), lambda i, j: (i % tiles_per_seq, 0)),
            pl.BlockSpec((PROJ_TM, RET_HEAD_QK), lambda i, j: (i % tiles_per_seq, 0)),
        ],
        out_specs=(pl.BlockSpec((PROJ_QK_TN // PROJ_GROUP, PROJ_TM, PROJ_GROUP), lambda i, j: (j, i, 0)),
                   pl.BlockSpec((PROJ_VG_TN // PROJ_GROUP, PROJ_TM, PROJ_GROUP), lambda i, j: (j, i, 0))),
        scratch_shapes=[pltpu.VMEM((PROJ_TM, D_MODEL), jnp.bfloat16),
                        pltpu.VMEM((1, PROJ_TM, D_MODEL), jnp.float32),
                        pltpu.SemaphoreType.DMA((1, PROJ_TM // ROW_CHUNK))],
        compiler_params=_params("arbitrary", "arbitrary"),
        name="ret_projection",
    )(x2d, mod, norm_g, w_in, w_in, cos, sin)


def _scan_kernel(q_ref, k_ref, v_ref, g_ref, mask_ref, eps_ref, kdec_ref, cdec_ref, o_ref, state_ref):
    @pl.when(pl.program_id(2) == 0)
    def _():
        state_ref[...] = jnp.zeros_like(state_ref)

    mask = mask_ref[...]
    row_eps = eps_ref[...]
    k_dec = kdec_ref[...]
    chunk_dec = cdec_ref[...]
    state = state_ref[...]
    for c in range(q_ref.shape[0] // RET_CHUNK):
        rows = slice(c * RET_CHUNK, (c + 1) * RET_CHUNK)
        qc = q_ref[rows, :]
        kc = k_ref[rows, :]
        vc = jnp.concatenate([v_ref[grp, rows, :] for grp in range(v_ref.shape[0])], axis=1)
        gc = jnp.concatenate([g_ref[grp, rows, :] for grp in range(g_ref.shape[0])], axis=1)
        scores = lax.dot_general(qc, kc, (((1,), (1,)), ((), ())), preferred_element_type=jnp.float32)
        scores = (scores * mask).astype(jnp.bfloat16)
        y = (jnp.dot(scores, vc, preferred_element_type=jnp.float32)
             + jnp.dot(qc, state.astype(jnp.bfloat16), preferred_element_type=jnp.float32))
        k_decayed = (kc.astype(jnp.float32) * k_dec).astype(jnp.bfloat16)
        state = chunk_dec * state + lax.dot_general(
            k_decayed, vc, (((0,), (0,)), ((), ())), preferred_element_type=jnp.float32)
        mu = jnp.mean(y, axis=-1, keepdims=True)
        yc = y - mu
        var = jnp.mean(yc * yc, axis=-1, keepdims=True)
        yn = yc * lax.rsqrt(var + row_eps)
        z = (jax.nn.silu(gc.astype(jnp.float32)) * yn).astype(o_ref.dtype)
        for grp in range(o_ref.shape[0]):
            o_ref[grp, rows, :] = z[:, grp * PROJ_GROUP:(grp + 1) * PROJ_GROUP]
    state_ref[...] = state


def _ret_scan(proj_qk, proj_vg, batch, seq, mask, row_eps, k_dec, chunk_dec):
    assert RET_HEAD_QK == PROJ_GROUP
    n_tok = proj_qk.shape[1]
    steps = seq // RET_TS
    v_groups = RET_HEAD_V // PROJ_GROUP
    k_off = RET_HEADS
    g_off = RET_HEADS
    return pl.pallas_call(
        _scan_kernel,
        out_shape=jax.ShapeDtypeStruct((RET_V_DIM // PROJ_GROUP, n_tok, PROJ_GROUP), jnp.bfloat16),
        grid=(batch, RET_HEADS, steps),
        in_specs=[
            pl.BlockSpec((None, RET_TS, PROJ_GROUP), lambda b, h, t: (h, b * steps + t, 0)),
            pl.BlockSpec((None, RET_TS, PROJ_GROUP), lambda b, h, t: (k_off + h, b * steps + t, 0)),
            pl.BlockSpec((v_groups, RET_TS, PROJ_GROUP), lambda b, h, t: (h, b * steps + t, 0)),
            pl.BlockSpec((v_groups, RET_TS, PROJ_GROUP), lambda b, h, t: (g_off + h, b * steps + t, 0)),
            pl.BlockSpec((None, RET_CHUNK, RET_CHUNK), lambda b, h, t: (h, 0, 0)),
            pl.BlockSpec((None, RET_CHUNK, 1), lambda b, h, t: (h, 0, 0)),
            pl.BlockSpec((None, RET_CHUNK, 1), lambda b, h, t: (h, 0, 0)),
            pl.BlockSpec((None, 1, 1), lambda b, h, t: (h, 0, 0)),
        ],
        out_specs=pl.BlockSpec((v_groups, RET_TS, PROJ_GROUP), lambda b, h, t: (h, b * steps + t, 0)),
        scratch_shapes=[pltpu.VMEM((RET_HEAD_QK, RET_HEAD_V), jnp.float32)],
        compiler_params=_params("parallel", "parallel", "arbitrary"),
        name="ret_scan",
    )(proj_qk, proj_qk, proj_vg, proj_vg, mask, row_eps, k_dec, chunk_dec)


def _out_kernel(z_ref, w_ref, x_ref, gate_ref, o_ref, wb_ref):
    j = pl.program_id(1)

    @pl.when(pl.program_id(0) == 0)
    def _():
        wb_ref[j] = w_ref[...].astype(jnp.bfloat16)

    y = None
    for grp in range(z_ref.shape[0]):
        part = jnp.dot(z_ref[grp], wb_ref[j, grp * PROJ_GROUP:(grp + 1) * PROJ_GROUP, :],
                       preferred_element_type=jnp.float32)
        y = part if y is None else y + part
    o_ref[...] = x_ref[...] + gate_ref[0] * y


def _ret_output(z, x2d, mod, mod_idx, seq, w_out, layer):
    n_tok = x2d.shape[0]
    tiles_per_seq = seq // OUT_TM
    n_col = D_MODEL // OUT_TN
    gate_off = 2 * D_MODEL // OUT_TN
    return pl.pallas_call(
        _out_kernel,
        out_shape=jax.ShapeDtypeStruct(x2d.shape, jnp.float32),
        grid=(n_tok // OUT_TM, n_col),
        in_specs=[
            pl.BlockSpec((RET_V_DIM // PROJ_GROUP, OUT_TM, PROJ_GROUP), lambda i, j: (0, i, 0)),
            pl.BlockSpec((None, RET_V_DIM, OUT_TN), lambda i, j: (layer, 0, jnp.where(i == 0, j, n_col - 1))),
            pl.BlockSpec((OUT_TM, OUT_TN), lambda i, j: (i, j)),
            pl.BlockSpec((1, 1, OUT_TN), lambda i, j: (mod_idx + i // tiles_per_seq, 0, gate_off + j)),
        ],
        out_specs=pl.BlockSpec((OUT_TM, OUT_TN), lambda i, j: (i, j)),
        scratch_shapes=[pltpu.VMEM((n_col, RET_V_DIM, OUT_TN), jnp.bfloat16)],
        compiler_params=_params("arbitrary", "arbitrary"),
        name="ret_output",
    )(z, w_out, x2d, mod)


def _rotary_tables(seq):
    inv = ROPE_BASE ** (-np.arange(0, RET_HEAD_QK, 2, dtype=np.float64) / RET_HEAD_QK)
    ang = np.arange(seq, dtype=np.float64)[:, None] * inv[None, :]
    cos = np.repeat(np.cos(ang), 2, axis=1)
    sin = np.stack([-np.sin(ang), np.sin(ang)], axis=-1).reshape(seq, RET_HEAD_QK)
    return cos.astype(np.float32), sin.astype(np.float32)


def _decay_tables():
    gamma = 1.0 - 2.0 ** (-5.0 - np.arange(RET_HEADS, dtype=np.float64))
    log_g = np.log(gamma)
    j = np.arange(RET_CHUNK, dtype=np.float64)
    causal = j[:, None] >= j[None, :]
    mask = np.where(causal[None], np.exp(-(j[None, None, :] + 1.0) * log_g[:, None, None]), 0.0)
    row_eps = (EPS * np.exp(-2.0 * (j[None, :] + 1.0) * log_g[:, None]))[:, :, None]
    k_dec = np.exp((RET_CHUNK - 1.0 - j[None, :]) * log_g[:, None])[:, :, None]
    chunk_dec = np.exp(RET_CHUNK * log_g)[:, None, None]
    return tuple(t.astype(np.float32) for t in (mask, row_eps, k_dec, chunk_dec))


@jax.jit
def kernel(x, c, ada_w, ada_b, norm_g, pool_w, pool_scale, ret_w_in, ret_w_out, mlp_w1, mlp_w2, final_g):
    batch, seq, d = x.shape
    assert d == D_MODEL
    assert seq % max(POOL_TM, MLP_TM, PROJ_TM, OUT_TM, RET_TS) == 0

    ada_w = ada_w.reshape(DEPTH * 2, d, 3 * d)
    ada_b = ada_b.reshape(DEPTH * 2, 1, 3 * d)
    c_lanes = jnp.broadcast_to(c[:, :, None], (batch, d, LANES))
    mod, cond = _ada_first_layer(c_lanes, ada_w, ada_b)

    cos, sin = _rotary_tables(seq)
    mask, row_eps, k_dec, chunk_dec = _decay_tables()
    norm_rows = norm_g.reshape(DEPTH * 2, 1, d)
    scale_rows = pool_scale.reshape(-1, 1, d)

    x2d = x.reshape(batch * seq, d)
    for i in range(DEPTH):
        mod = mod.reshape(2 * batch, 1, 3 * d)
        mix_idx, mlp_idx = 0, batch
        if i % 2 == 0:
            x2d = _pool_layer(x2d, mod, mix_idx, seq, norm_rows, 2 * i, pool_w, scale_rows, i // 2)
        else:
            proj_qk, proj_vg = _ret_projection(x2d, mod, mix_idx, seq, norm_rows, 2 * i, ret_w_in, i // 2, cos, sin)
            z = _ret_scan(proj_qk, proj_vg, batch, seq, mask, row_eps, k_dec, chunk_dec)
            x2d = _ret_output(z, x2d, mod, mix_idx, seq, ret_w_out, i // 2)
        if i < DEPTH - 1:
            x2d, mod = _mlp_layer(x2d, mod, mlp_idx, seq, norm_rows, 2 * i + 1, mlp_w1, mlp_w2, i,
                                  ada=(ada_w, ada_b, cond))
        else:
            x2d = _mlp_layer(x2d, mod, mlp_idx, seq, norm_rows, 2 * i + 1, mlp_w1, mlp_w2, i,
                             final_g=final_g.reshape(1, d))
    return x2d.reshape(batch, seq, d)
```
